```python
import math
import jax, jax.numpy as jnp
from jax import lax
import numpy as np

D_MODEL = 1024
BATCH = 2
SEQ = 8192
DEPTH = 2
DEC_BATCH = 128
DEC_SEQ = 8
PAST_LEN = 16384
PAGE_SIZE = 128

N_META = 16
EPS = 1e-6
A_HEADS = 4
A_DK = 128
A_DV = 128
A_CONV = 4
A_CHUNK = 64
A_WIDTH = A_HEADS * A_DV
A_QKV = 2 * A_HEADS * A_DK + A_HEADS * A_DV
B_HEADS = 4
B_DIM = 128
B_WIDTH = B_HEADS * B_DIM
SB_BLOCK = 128
SB_BIAS_INIT = -8.0
C_HEADS = 16
C_KV_HEADS = 4
C_DIM = 64
C_WIDTH = C_HEADS * C_DIM
WINDOW = 128
N_BUCKETS = 32
MAX_DISTANCE = 128

L0_SPLITS = (A_QKV, A_WIDTH, A_HEADS, A_HEADS, B_WIDTH, B_WIDTH, B_WIDTH, B_WIDTH)
L0_IN = A_QKV + A_WIDTH + 2 * A_HEADS + 4 * B_WIDTH
L0_OUT = A_WIDTH + B_WIDTH
L1_SPLITS = (C_WIDTH, C_KV_HEADS * C_DIM, C_KV_HEADS * C_DIM, C_WIDTH)
L1_IN = 2 * C_WIDTH + 2 * C_KV_HEADS * C_DIM

kernel_name = 'hybrid_gdn_stickbreak_swa_step'


def rms_norm(x, g):
    xf = x.astype(jnp.float32)
    y = xf * lax.rsqrt(jnp.mean(xf * xf, axis=-1, keepdims=True) + EPS)
    return (y * g.astype(jnp.float32)).astype(x.dtype)


def l2_norm(x):
    xf = x.astype(jnp.float32)
    return xf * lax.rsqrt(jnp.sum(xf * xf, axis=-1, keepdims=True) + EPS)


def split_cols(x, sizes):
    offs = [sum(sizes[:i + 1]) for i in range(len(sizes) - 1)]
    return jnp.split(x, offs, axis=-1)


def t5_bucket(d):
    d = jnp.maximum(d, 0)
    exact = N_BUCKETS // 2
    large = exact + (jnp.log(jnp.maximum(d, 1).astype(jnp.float32) / exact)
                     / math.log(MAX_DISTANCE / exact) * (N_BUCKETS - exact)).astype(jnp.int32)
    return jnp.where(d < exact, d, jnp.minimum(large, N_BUCKETS - 1))


def causal_conv(u, buf, w):
    full = jnp.concatenate([buf.astype(u.dtype), u], axis=1)
    y = lax.conv_general_dilated(full, w[:, None, :].astype(full.dtype), window_strides=(1,), padding='VALID',
                                 dimension_numbers=('NWC', 'WIO', 'NWC'), feature_group_count=full.shape[-1])
    return jax.nn.silu(y), full[:, -(A_CONV - 1):]


def gated_delta_chunked(q, k, v, g, beta, S0, chunk):
    B, L, H, _ = q.shape
    N = L // chunk

    def to_chunks(x):
        x = x.reshape((B, N, chunk) + x.shape[2:])
        return jnp.swapaxes(jnp.moveaxis(x, 1, 0), 2, 3)

    q, k, v, g, beta = [to_chunks(t) for t in (q, k, v, g, beta)]
    gc = jnp.cumsum(g, axis=-1)
    incl = jnp.tril(jnp.ones((chunk, chunk), bool))
    strict = jnp.tril(jnp.ones((chunk, chunk), bool), -1)
    decay = jnp.exp(jnp.where(incl, gc[..., :, None] - gc[..., None, :], -jnp.inf))
    kb = k * beta[..., None]
    a_kk = jnp.where(strict, jnp.einsum('nbhid,nbhjd->nbhij', kb, k) * decay, 0.0)
    m = jnp.eye(chunk, dtype=jnp.float32) + a_kk
    u = lax.linalg.triangular_solve(m, v * beta[..., None], left_side=True, lower=True, unit_diagonal=True)
    w = lax.linalg.triangular_solve(m, kb * jnp.exp(gc)[..., None], left_side=True, lower=True, unit_diagonal=True)
    a_qk = jnp.einsum('nbhid,nbhjd->nbhij', q, k) * decay
    q_dec = q * jnp.exp(gc)[..., None]
    k_dec = k * jnp.exp(gc[..., -1:] - gc)[..., None]
    g_tot = jnp.exp(gc[..., -1])[..., None, None]

    def step(S, xs):
        u_n, w_n, aqk_n, qd_n, kd_n, gt_n = xs
        v_new = u_n - jnp.einsum('bhck,bhkv->bhcv', w_n, S)
        o = jnp.einsum('bhck,bhkv->bhcv', qd_n, S) + jnp.einsum('bhij,bhjv->bhiv', aqk_n, v_new)
        S = S * gt_n + jnp.einsum('bhck,bhcv->bhkv', kd_n, v_new)
        return S, o

    S, o = lax.scan(step, S0, (u, w, a_qk, q_dec, k_dec, g_tot))
    o = jnp.moveaxis(jnp.swapaxes(o, 2, 3), 0, 1).reshape(B, L, H, -1)
    return o, S


def gated_delta_mixer(qkv, z, b, a, conv_buf, S0, segments, conv_w, a_log, dt_bias, o_norm):
    Bn, L, _ = qkv.shape
    c, new_buf = causal_conv(qkv, conv_buf, conv_w)
    cq, ck, cv = split_cols(c, (A_HEADS * A_DK, A_HEADS * A_DK, A_HEADS * A_DV))
    q = l2_norm(cq.reshape(Bn, L, A_HEADS, A_DK)) * (A_DK ** -0.5)
    k = l2_norm(ck.reshape(Bn, L, A_HEADS, A_DK))
    v = cv.reshape(Bn, L, A_HEADS, A_DV).astype(jnp.float32)
    beta = jax.nn.sigmoid(b.astype(jnp.float32))
    g = -jnp.exp(a_log.astype(jnp.float32)) * jax.nn.softplus(a.astype(jnp.float32) + dt_bias.astype(jnp.float32))
    S = S0.astype(jnp.float32)
    outs = []
    for start, length, chunk in segments:
        o, S = gated_delta_chunked(q[:, start:start + length], k[:, start:start + length],
                                   v[:, start:start + length], g[:, start:start + length],
                                   beta[:, start:start + length], S, chunk)
        outs.append(o)
    o = jnp.concatenate(outs, axis=1)
    o = rms_norm(o, o_norm) * jax.nn.silu(z.astype(jnp.float32).reshape(Bn, L, A_HEADS, A_DV))
    return o.reshape(Bn, L, A_WIDTH).astype(qkv.dtype), new_buf, S


def stick_breaking(q, k, v, allow, sb_bias):
    z = jnp.einsum('...qhd,...khd->...hqk', q, k).astype(jnp.float32) * (B_DIM ** -0.5)
    z = z + sb_bias.astype(jnp.float32)[:, None, None]
    log_beta = jax.nn.log_sigmoid(z)
    log_keep = jnp.where(allow, jax.nn.log_sigmoid(-z), 0.0)
    log_after = lax.cumsum(log_keep, axis=z.ndim - 1, reverse=True) - log_keep
    wgt = jnp.where(allow, jnp.exp(log_beta + log_after), 0.0)
    return jnp.einsum('...hqk,...khd->...qhd', wgt, v.astype(jnp.float32)).astype(q.dtype)


def stick_breaking_prompt(q, k, v, sb_bias):
    Bn, L = q.shape[:2]
    nb = -(-L // SB_BLOCK)
    Lp = nb * SB_BLOCK
    pad = ((0, 0), (0, Lp - L), (0, 0), (0, 0))
    q, k, v = [jnp.pad(t, pad) for t in (q, k, v)]
    qb = jnp.moveaxis(q.reshape(Bn, nb, SB_BLOCK, B_HEADS, B_DIM), 1, 0)
    key_pos = jnp.arange(Lp)

    def one_block(args):
        i, qi = args
        q_pos = i * SB_BLOCK + jnp.arange(SB_BLOCK)
        return stick_breaking(qi, k, v, key_pos[None, :] < q_pos[:, None], sb_bias)

    o = lax.map(one_block, (jnp.arange(nb), qb))
    return jnp.moveaxis(o, 0, 1).reshape(Bn, Lp, B_HEADS, B_DIM)[:, :L]


def stick_breaking_sample(q, k_new, v_new, sb_bias, cache_k, cache_v, page_table):
    S = q.shape[1]
    P = page_table.shape[1] * PAGE_SIZE
    allow = jnp.arange(P + S)[None, :] < (P + jnp.arange(S))[:, None]

    def one_seq(args):
        qi, ki, vi, pages = args
        kp = cache_k[pages].reshape(P, B_HEADS, B_DIM).astype(ki.dtype)
        vp = cache_v[pages].reshape(P, B_HEADS, B_DIM).astype(vi.dtype)
        return stick_breaking(qi, jnp.concatenate([kp, ki], 0), jnp.concatenate([vp, vi], 0), allow, sb_bias)

    return lax.map(one_seq, (q, k_new, v_new, page_table))


def swa_core(q, k, v, bias, allow, sinks):
    G = C_HEADS // C_KV_HEADS
    qg = q.reshape(q.shape[:-2] + (C_KV_HEADS, G, C_DIM))
    s = jnp.einsum('...qngd,...snd->...ngqs', qg, k).astype(jnp.float32) * (C_DIM ** -0.5)
    s = s + bias.reshape(bias.shape[:-3] + (C_KV_HEADS, G) + bias.shape[-2:]).astype(jnp.float32)
    s = jnp.where(allow[..., None, None, :, :], s, -jnp.inf)
    sink = sinks.astype(jnp.float32).reshape(C_KV_HEADS, G, 1, 1)
    mx = jnp.maximum(jnp.max(s, axis=-1, keepdims=True), sink)
    p = jnp.exp(s - mx)
    denom = jnp.sum(p, axis=-1, keepdims=True) + jnp.exp(sink - mx)
    o = jnp.einsum('...ngqs,...snd->...qngd', p / denom, v.astype(jnp.float32))
    return o.reshape(q.shape).astype(q.dtype)


def swa_prompt(q, k, v, rel_bias, sinks):
    Bn, L = q.shape[:2]
    nb = -(-L // WINDOW)
    Lp = nb * WINDOW

    def pad(t):
        return jnp.pad(t, ((0, 0), (0, Lp - L), (0, 0), (0, 0)))

    qb = pad(q).reshape(Bn, nb, WINDOW, C_HEADS, C_DIM)

    def band(t):
        tb = pad(t).reshape(Bn, nb, WINDOW, C_KV_HEADS, C_DIM)
        prev = jnp.pad(tb, ((0, 0), (1, 0), (0, 0), (0, 0), (0, 0)))[:, :-1]
        meta = jnp.broadcast_to(t[:, None, :N_META], (Bn, nb, N_META, C_KV_HEADS, C_DIM))
        return jnp.concatenate([meta, prev, tb], axis=2)

    blk = jnp.arange(nb)[:, None] * WINDOW
    t_pos = blk + jnp.arange(WINDOW)[None, :]
    s_pos = jnp.concatenate([jnp.broadcast_to(jnp.arange(N_META)[None, :], (nb, N_META)),
                             blk - WINDOW + jnp.arange(2 * WINDOW)[None, :]], axis=1)
    d = t_pos[:, :, None] - s_pos[:, None, :]
    is_meta = jnp.arange(s_pos.shape[1]) < N_META
    allow = jnp.where(is_meta, d >= WINDOW, (d >= 0) & (d < WINDOW) & (s_pos[:, None, :] >= 0))
    bias = jnp.moveaxis(rel_bias[t5_bucket(d)], -1, -3)
    o = swa_core(qb, band(k), band(v), bias, allow, sinks)
    return o.reshape(Bn, Lp, C_HEADS, C_DIM)[:, :L]


def swa_sample(q, k, v, meta_k, meta_v, win_k, win_v, past_len, rel_bias, sinks):
    S = q.shape[1]
    n_win = win_k.shape[1]
    keys = jnp.concatenate([meta_k.astype(k.dtype), win_k.astype(k.dtype), k], axis=1)
    vals = jnp.concatenate([meta_v.astype(v.dtype), win_v.astype(v.dtype), v], axis=1)
    s_pos = jnp.concatenate([jnp.arange(N_META), past_len - n_win + jnp.arange(n_win), past_len + jnp.arange(S)])
    t_pos = past_len + jnp.arange(S)
    d = t_pos[:, None] - s_pos[None, :]
    is_meta = jnp.arange(s_pos.shape[0]) < N_META
    allow = jnp.where(is_meta, d >= WINDOW, (d >= 0) & (d < WINDOW))
    bias = jnp.moveaxis(rel_bias[t5_bucket(d)], -1, -3)
    return swa_core(q, keys, vals, bias, allow, sinks)


def layer_even(h, conv_buf, gdn_state, segments, sb_fn, norm0, w_in0, conv0, a_log0, dt_bias0, gdn_norm0,
               sb_bias0, w_out0):
    Bn, L, _ = h.shape
    u = rms_norm(h, norm0) @ w_in0
    a_qkv, a_z, a_b, a_a, b_q, b_k, b_v, b_z = split_cols(u, L0_SPLITS)
    o_a, new_conv, new_S = gated_delta_mixer(a_qkv, a_z, a_b, a_a, conv_buf, gdn_state, segments,
                                             conv0, a_log0, dt_bias0, gdn_norm0)
    b_q = b_q.reshape(Bn, L, B_HEADS, B_DIM)
    b_k = b_k.reshape(Bn, L, B_HEADS, B_DIM)
    b_v = b_v.reshape(Bn, L, B_HEADS, B_DIM)
    o_b = sb_fn(b_q, b_k, b_v, sb_bias0).reshape(Bn, L, B_WIDTH) * jax.nn.silu(b_z)
    h = h + jnp.concatenate([o_a, o_b], axis=-1) @ w_out0
    return h, b_k, b_v, new_conv, new_S


def layer_odd(h, attn_fn, norm1, w_in1, q_norm1, k_norm1, w_out1):
    Bn, L, _ = h.shape
    u = rms_norm(h, norm1) @ w_in1
    cq, ck, cv, cz = split_cols(u, L1_SPLITS)
    q = rms_norm(cq.reshape(Bn, L, C_HEADS, C_DIM), q_norm1)
    k = rms_norm(ck.reshape(Bn, L, C_KV_HEADS, C_DIM), k_norm1)
    v = cv.reshape(Bn, L, C_KV_HEADS, C_DIM)
    o = attn_fn(q, k, v).reshape(Bn, L, C_WIDTH) * jax.nn.silu(cz)
    return h + o @ w_out1, k, v


def setup_inputs(seed: int = 0) -> dict:
    key = jax.random.key(seed)
    ks = jax.random.split(key, 30)

    def nrm(k, shape, s=1.0):
        return s * jax.random.normal(k, shape, jnp.float32)

    n_pages = PAST_LEN // PAGE_SIZE
    n_phys = (DEC_BATCH * n_pages * 5) // 4
    perm = jax.random.permutation(ks[0], n_phys)
    page_table = perm[:DEC_BATCH * n_pages].reshape(DEC_BATCH, n_pages).astype(jnp.int32)
    dt = jnp.exp(jax.random.uniform(ks[1], (A_HEADS,), jnp.float32, math.log(1e-3), math.log(1e-1)))
    dt_bias = dt + jnp.log(-jnp.expm1(-dt))
    a_log = jnp.log(jax.random.uniform(ks[2], (A_HEADS,), jnp.float32, 1.0, 16.0))
    return {
        'x_prompt': nrm(ks[3], (BATCH, SEQ, D_MODEL)),
        'x_sample': nrm(ks[4], (DEC_BATCH, DEC_SEQ, D_MODEL)),
        'cache_sb_k': nrm(ks[5], (n_phys, PAGE_SIZE, B_HEADS, B_DIM)),
        'cache_sb_v': nrm(ks[6], (n_phys, PAGE_SIZE, B_HEADS, B_DIM)),
        'page_table': page_table,
        'state_gdn': nrm(ks[7], (DEC_BATCH, A_HEADS, A_DK, A_DV)),
        'state_gdn_conv': nrm(ks[8], (DEC_BATCH, A_CONV - 1, A_QKV)),
        'cache_swa_k': nrm(ks[9], (DEC_BATCH, WINDOW, C_KV_HEADS, C_DIM)),
        'cache_swa_v': nrm(ks[10], (DEC_BATCH, WINDOW, C_KV_HEADS, C_DIM)),
        'cache_swa_meta_k': nrm(ks[11], (DEC_BATCH, N_META, C_KV_HEADS, C_DIM)),
        'cache_swa_meta_v': nrm(ks[12], (DEC_BATCH, N_META, C_KV_HEADS, C_DIM)),
        'meta_tokens': nrm(ks[13], (N_META, D_MODEL)),
        'rel_bias': nrm(ks[14], (N_BUCKETS, C_HEADS), 0.5),
        'norm0': 1.0 + nrm(ks[15], (D_MODEL,), 0.05),
        'w_in0': nrm(ks[16], (D_MODEL, L0_IN), D_MODEL ** -0.5),
        'conv0': nrm(ks[17], (A_CONV, A_QKV), A_CONV ** -0.5),
        'a_log0': a_log,
        'dt_bias0': dt_bias,
        'gdn_norm0': 1.0 + nrm(ks[18], (A_DV,), 0.05),
        'sb_bias0': SB_BIAS_INIT + nrm(ks[26], (B_HEADS,), 0.5),
        'w_out0': nrm(ks[19], (L0_OUT, D_MODEL), L0_OUT ** -0.5),
        'norm1': 1.0 + nrm(ks[20], (D_MODEL,), 0.05),
        'w_in1': nrm(ks[21], (D_MODEL, L1_IN), D_MODEL ** -0.5),
        'q_norm1': 1.0 + nrm(ks[22], (C_DIM,), 0.05),
        'k_norm1': 1.0 + nrm(ks[23], (C_DIM,), 0.05),
        'sinks1': nrm(ks[24], (C_HEADS,)),
        'w_out1': nrm(ks[25], (C_WIDTH, D_MODEL), C_WIDTH ** -0.5),
    }


def reference(x_prompt, x_sample, cache_sb_k, cache_sb_v, page_table, state_gdn, state_gdn_conv,
              cache_swa_k, cache_swa_v, cache_swa_meta_k, cache_swa_meta_v, meta_tokens, rel_bias,
              norm0, w_in0, conv0, a_log0, dt_bias0, gdn_norm0, sb_bias0, w_out0,
              norm1, w_in1, q_norm1, k_norm1, sinks1, w_out1):
    bp, n_real, _ = x_prompt.shape
    dec_seq = x_sample.shape[1]
    past_len = page_table.shape[1] * PAGE_SIZE
    hp = jnp.concatenate([jnp.broadcast_to(meta_tokens.astype(x_prompt.dtype)[None], (bp, N_META, D_MODEL)),
                          x_prompt], axis=1)
    hs = x_sample
    l0 = (norm0, w_in0, conv0, a_log0, dt_bias0, gdn_norm0, sb_bias0, w_out0)
    l1 = (norm1, w_in1, q_norm1, k_norm1, w_out1)
    prompt_segments = ((0, N_META, N_META), (N_META, n_real, A_CHUNK))
    sample_segments = ((0, dec_seq, dec_seq),)

    def sb_sample(q, k, v, sb_bias):
        return stick_breaking_sample(q, k, v, sb_bias, cache_sb_k, cache_sb_v, page_table)

    def swa_p(q, k, v):
        return swa_prompt(q, k, v, rel_bias, sinks1)

    def swa_s(q, k, v):
        return swa_sample(q, k, v, cache_swa_meta_k, cache_swa_meta_v, cache_swa_k, cache_swa_v,
                          past_len, rel_bias, sinks1)

    for layer in range(DEPTH):
        if layer % 2 == 0:
            conv_zero = jnp.zeros((bp, A_CONV - 1, A_QKV), hp.dtype)
            gdn_zero = jnp.zeros((bp, A_HEADS, A_DK, A_DV), jnp.float32)
            hp, sbk_p, sbv_p, conv_p, gdn_p = layer_even(hp, conv_zero, gdn_zero, prompt_segments,
                                                         stick_breaking_prompt, *l0)
            hs, sbk_s, sbv_s, conv_s, gdn_s = layer_even(hs, state_gdn_conv, state_gdn, sample_segments,
                                                         sb_sample, *l0)
        else:
            hp, ck_p, cv_p = layer_odd(hp, swa_p, *l1)
            hs, ck_s, cv_s = layer_odd(hs, swa_s, *l1)

    y_prompt = hp[:, N_META:]
    y_sample = hs
    swk_p = ck_p[:, -WINDOW:]
    swv_p = cv_p[:, -WINDOW:]
    swmk_p = ck_p[:, :N_META]
    swmv_p = cv_p[:, :N_META]
    swk_s = jnp.concatenate([cache_swa_k.astype(ck_s.dtype), ck_s], axis=1)[:, -WINDOW:]
    swv_s = jnp.concatenate([cache_swa_v.astype(cv_s.dtype), cv_s], axis=1)[:, -WINDOW:]
    return (y_prompt, y_sample, sbk_p, sbv_p, sbk_s, sbv_s, gdn_p, gdn_s, conv_p, conv_s,
            swk_p, swv_p, swmk_p, swmv_p, swk_s, swv_s)
```

```python
import functools
import math

import jax
import jax.numpy as jnp
from jax import lax
from jax.experimental import pallas as pl
from jax.experimental.pallas import tpu as pltpu

F32 = jnp.float32
BF16 = jnp.bfloat16
HIGHEST = lax.Precision.HIGHEST

D_MODEL = 1024
N_META = 16
EPS = 1e-6
A_HEADS = 4
A_DK = 128
A_DV = 128
A_CONV = 4
A_CHUNK = 64
A_WIDTH = A_HEADS * A_DV
A_QKV = 2 * A_HEADS * A_DK + A_HEADS * A_DV
B_HEADS = 4
B_DIM = 128
B_WIDTH = B_HEADS * B_DIM
C_HEADS = 16
C_KV_HEADS = 4
C_DIM = 64
C_WIDTH = C_HEADS * C_DIM
C_KV_WIDTH = C_KV_HEADS * C_DIM
WINDOW = 128
N_BUCKETS = 32
MAX_DISTANCE = 128
PAGE_SIZE = 128

LANES = 128
BLK = 128
SEQ_OFF = A_CHUNK - N_META
NEG = -1e30
PAGES_PER_STEP = 8
VMEM_LIMIT = 56 * 1024 * 1024

NN = (((1,), (0,)), ((), ()))
NT = (((1,), (1,)), ((), ()))
TN = (((0,), (0,)), ((), ()))


def _dot(a, b, dims=NN):
    return lax.dot_general(a, b, dims, preferred_element_type=F32)


def _dot_hi(a, b, dims=NN):
    return lax.dot_general(a, b, dims, precision=HIGHEST, preferred_element_type=F32)


def _silu(x):
    return x * (1.0 / (1.0 + jnp.exp(-x)))


def _softplus(x):
    return jnp.maximum(x, 0.0) + jnp.log(1.0 + jnp.exp(-jnp.abs(x)))


def _split_bf16(x):
    hi = x.astype(BF16)
    lo = (x - hi.astype(F32)).astype(BF16)
    return hi, lo


def _params(*sem):
    return pltpu.CompilerParams(dimension_semantics=sem, vmem_limit_bytes=VMEM_LIMIT)


def _rows(tr, width, *, squeeze=False):
    return pl.BlockSpec((tr, width), lambda i: (i, 0))


def _resident(shape):
    nd = len(shape)
    return pl.BlockSpec(shape, lambda *_: (0,) * nd)


_L0_COLS = (("qkv", A_QKV), ("z", A_WIDTH), ("bq", B_WIDTH), ("bk", B_WIDTH), ("bv", B_WIDTH),
            ("bz", B_WIDTH))


def _inproj0_kernel(x_ref, g_ref, w_ref, wba_ref, qkv_ref, z_ref, bq_ref, bk_ref, bv_ref, bz_ref,
                    ba_ref, bkh_ref, bvh_ref):
    x = x_ref[...]
    xn = x * lax.rsqrt(jnp.mean(x * x, axis=-1, keepdims=True) + EPS) * g_ref[...]
    xb = xn.astype(BF16)
    off = 0
    outs = {}
    for (name, width), ref in zip(_L0_COLS, (qkv_ref, z_ref, bq_ref, bk_ref, bv_ref, bz_ref)):
        u = _dot(xb, w_ref[:, off:off + width])
        outs[name] = u
        ref[...] = u.astype(ref.dtype)
        off += width
    bkh_ref[...] = outs["bk"].astype(BF16)
    bvh_ref[...] = outs["bv"].astype(BF16)
    ba_ref[...] = _dot_hi(xn, wba_ref[...])


def _inproj0(x, g, w_main, w_ba, tr):
    n = x.shape[0]
    widths = [w for _, w in _L0_COLS]
    out_shape = [jax.ShapeDtypeStruct((n, A_QKV), F32), jax.ShapeDtypeStruct((n, A_WIDTH), F32),
                 jax.ShapeDtypeStruct((n, B_WIDTH), BF16), jax.ShapeDtypeStruct((n, B_WIDTH), F32),
                 jax.ShapeDtypeStruct((n, B_WIDTH), F32), jax.ShapeDtypeStruct((n, B_WIDTH), F32),
                 jax.ShapeDtypeStruct((n, LANES), F32),
                 jax.ShapeDtypeStruct((n, B_WIDTH), BF16), jax.ShapeDtypeStruct((n, B_WIDTH), BF16)]
    out_specs = [_rows(tr, w) for w in widths] + [_rows(tr, LANES), _rows(tr, B_WIDTH), _rows(tr, B_WIDTH)]
    return pl.pallas_call(
        _inproj0_kernel,
        grid=(n // tr,),
        in_specs=[_rows(tr, D_MODEL), _resident((1, D_MODEL)), _resident(w_main.shape), _resident(w_ba.shape)],
        out_specs=out_specs,
        out_shape=out_shape,
        compiler_params=_params("parallel"),
        name="inproj0",
    )(x, g, w_main, w_ba)


def _gdn_chunk(c, ba, z, alog_row, dtb_row, onorm_row, s_list):
    C = c.shape[0]
    beta_all = 1.0 / (1.0 + jnp.exp(-ba))
    g_all = -jnp.exp(alog_row) * _softplus(ba + dtb_row)
    ri = lax.broadcasted_iota(jnp.int32, (C, C), 0)
    ci = lax.broadcasted_iota(jnp.int32, (C, C), 1)
    incl = ri >= ci
    strict = ri > ci
    eye = ri == ci
    eye_f = eye.astype(F32)
    ones = jnp.ones((C, C), F32)
    gc_all = _dot_hi(incl.astype(F32), g_all)
    gl_all = gc_all[C - 1:C, :]
    outs, new_states = [], []
    for h in range(A_HEADS):
        q = c[:, h * A_DK:(h + 1) * A_DK]
        k = c[:, (A_HEADS + h) * A_DK:(A_HEADS + h + 1) * A_DK]
        v = c[:, 2 * A_HEADS * A_DK + h * A_DV:2 * A_HEADS * A_DK + (h + 1) * A_DV]
        q = q * lax.rsqrt(jnp.sum(q * q, axis=-1, keepdims=True) + EPS) * (A_DK ** -0.5)
        k = k * lax.rsqrt(jnp.sum(k * k, axis=-1, keepdims=True) + EPS)
        beta = beta_all[:, h:h + 1]
        gc = gc_all[:, A_HEADS + h:A_HEADS + h + 1]
        gl = gl_all[:, A_HEADS + h:A_HEADS + h + 1]
        gcb = jnp.broadcast_to(gc, (C, C))
        gcr = _dot_hi(ones, jnp.where(eye, gcb, 0.0))
        decay = jnp.exp(jnp.where(incl, gcb - gcr, NEG))
        kb = k * beta
        a_kk = jnp.where(strict, _dot_hi(kb, k, NT) * decay, 0.0)
        p = -a_kk
        t = eye_f + p
        n = 2
        while n < C:
            p = _dot_hi(p, p)
            t = t + _dot_hi(t, p)
            n *= 2
        eg = jnp.exp(gc)
        u = _dot_hi(t, v * beta)
        w = _dot_hi(t, kb * eg)
        a_qk = _dot_hi(q, k, NT) * decay
        s = s_list[h]
        v_new = u - _dot_hi(w, s)
        o = _dot_hi(q * eg, s) + _dot_hi(a_qk, v_new)
        kd = k * jnp.exp(gl - gc)
        new_states.append(s * jnp.exp(gl) + _dot_hi(kd, v_new, TN))
        o = o * lax.rsqrt(jnp.mean(o * o, axis=-1, keepdims=True) + EPS) * onorm_row
        outs.append(o * _silu(z[:, h * A_DV:(h + 1) * A_DV]))
    return outs, new_states


def _conv_silu(xbuf_ref, base, rows, convw_ref):
    acc = None
    for j in range(A_CONV):
        term = xbuf_ref[pl.ds(base - (A_CONV - 1) + j, rows), :] * convw_ref[j:j + 1, :]
        acc = term if acc is None else acc + term
    return _silu(acc)


_XPAD = 8


def _gdn_prompt_kernel(qkv_ref, z_ref, ba_ref, convw_ref, alog_ref, dtb_ref, onorm_ref,
                       o_ref, s_out_ref, xbuf_ref, s_ref):
    n = pl.program_id(1)
    last = pl.num_programs(1) - 1
    C = A_CHUNK

    @pl.when(n == 0)
    def _():
        xbuf_ref[0:_XPAD, :] = jnp.zeros((_XPAD, A_QKV), F32)
        s_ref[...] = jnp.zeros_like(s_ref)

    xbuf_ref[_XPAD:_XPAD + C, :] = qkv_ref[0]

    def run(base, rows):
        c = _conv_silu(xbuf_ref, _XPAD + base, rows, convw_ref)
        outs, new_s = _gdn_chunk(c, ba_ref[0, base:base + rows, :], z_ref[0, base:base + rows, :],
                                 alog_ref[...], dtb_ref[...], onorm_ref[...],
                                 [s_ref[h] for h in range(A_HEADS)])
        for h in range(A_HEADS):
            o_ref[0, base:base + rows, h * A_DV:(h + 1) * A_DV] = outs[h]
            s_ref[h] = new_s[h]

    @pl.when(n == 0)
    def _():
        o_ref[0, 0:SEQ_OFF, :] = jnp.zeros((SEQ_OFF, A_WIDTH), F32)
        run(SEQ_OFF, N_META)

    @pl.when(jnp.logical_and(n > 0, n < last))
    def _():
        run(0, C)

    @pl.when(n == last)
    def _():
        o_ref[0] = jnp.zeros((C, A_WIDTH), F32)

    @pl.when(n == last - 1)
    def _():
        s_out_ref[0] = s_ref[...]

    xbuf_ref[0:_XPAD, :] = xbuf_ref[C:C + _XPAD, :]


def _gdn_prompt(qkv, z, ba, convw, alog_row, dtb_row, onorm_row):
    bsz, lp, _ = qkv.shape
    C = A_CHUNK
    nblk = lp // C
    blk = lambda w: pl.BlockSpec((1, C, w), lambda b, n: (b, n, 0))
    return pl.pallas_call(
        _gdn_prompt_kernel,
        grid=(bsz, nblk),
        in_specs=[blk(A_QKV), blk(A_WIDTH), blk(LANES), _resident(convw.shape), _resident((1, LANES)),
                  _resident((1, LANES)), _resident((1, A_DV))],
        out_specs=[blk(A_WIDTH), pl.BlockSpec((1, A_HEADS, A_DK, A_DV), lambda b, n: (b, 0, 0, 0))],
        out_shape=[jax.ShapeDtypeStruct((bsz, lp, A_WIDTH), F32),
                   jax.ShapeDtypeStruct((bsz, A_HEADS, A_DK, A_DV), F32)],
        scratch_shapes=[pltpu.VMEM((_XPAD + C, A_QKV), F32), pltpu.VMEM((A_HEADS, A_DK, A_DV), F32)],
        compiler_params=_params("arbitrary", "arbitrary"),
        name="gdn_prompt",
    )(qkv, z, ba, convw, alog_row, dtb_row, onorm_row)


def _gdn_sample_kernel(qkv_ref, buf_ref, z_ref, ba_ref, s0_ref, convw_ref, alog_ref, dtb_ref, onorm_ref,
                       o_ref, s_out_ref, xbuf_ref):
    S = qkv_ref.shape[1]
    xbuf_ref[0:_XPAD, :] = jnp.zeros((_XPAD, A_QKV), F32)
    xbuf_ref[_XPAD - (A_CONV - 1):_XPAD, :] = buf_ref[0]
    xbuf_ref[_XPAD:_XPAD + S, :] = qkv_ref[0]
    c = _conv_silu(xbuf_ref, _XPAD, S, convw_ref)
    outs, new_s = _gdn_chunk(c, ba_ref[0], z_ref[0], alog_ref[...], dtb_ref[...], onorm_ref[...],
                             [s0_ref[0, h] for h in range(A_HEADS)])
    for h in range(A_HEADS):
        o_ref[0, :, h * A_DV:(h + 1) * A_DV] = outs[h]
        s_out_ref[0, h] = new_s[h]


def _gdn_sample(qkv, buf, z, ba, s0, convw, alog_row, dtb_row, onorm_row):
    db, S, _ = qkv.shape
    blk = lambda r, w: pl.BlockSpec((1, r, w), lambda b: (b, 0, 0))
    sblk = pl.BlockSpec((1, A_HEADS, A_DK, A_DV), lambda b: (b, 0, 0, 0))
    return pl.pallas_call(
        _gdn_sample_kernel,
        grid=(db,),
        in_specs=[blk(S, A_QKV), blk(A_CONV - 1, A_QKV), blk(S, A_WIDTH), blk(S, LANES), sblk,
                  _resident(convw.shape), _resident((1, LANES)), _resident((1, LANES)), _resident((1, A_DV))],
        out_specs=[blk(S, A_WIDTH), sblk],
        out_shape=[jax.ShapeDtypeStruct((db, S, A_WIDTH), F32),
                   jax.ShapeDtypeStruct((db, A_HEADS, A_DK, A_DV), F32)],
        scratch_shapes=[pltpu.VMEM((_XPAD + S, A_QKV), F32)],
        compiler_params=_params("parallel"),
        name="gdn_sample",
    )(qkv, buf, z, ba, s0, convw, alog_row, dtb_row, onorm_row)


def _suffix_matrix():
    j = lax.broadcasted_iota(jnp.int32, (BLK, 2 * BLK), 0)
    s = lax.broadcasted_iota(jnp.int32, (BLK, 2 * BLK), 1)
    return jnp.where(jnp.logical_or(s >= BLK, j > s), 1.0, 0.0).astype(BF16)


def _sb_block(s, bias, mask, u_ext, v_blk, carry, acc):
    zl = s * (B_DIM ** -0.5) + bias
    sp = _softplus(zl)
    log_keep = -sp
    log_beta = zl - sp
    if mask is not None:
        log_keep = jnp.where(mask, log_keep, 0.0)
    hi, lo = _split_bf16(log_keep)
    r = _dot(hi, u_ext) + _dot(lo, u_ext)
    w = jnp.exp(log_beta + r[:, :BLK] + carry)
    if mask is not None:
        w = jnp.where(mask, w, 0.0)
    acc = acc + _dot(w.astype(BF16), v_blk)
    carry = carry + r[:, BLK:]
    return carry, acc


def _sb_prompt_kernel(bias_ref, q_ref, k_ref, v_ref, bz_ref, u_ref, o_ref):
    h = pl.program_id(1)
    qi = pl.program_id(2)
    bias = bias_ref[h]
    q = q_ref[0]
    u_ext = u_ref[...]
    rows = lax.broadcasted_iota(jnp.int32, (BLK, BLK), 0)
    cols = lax.broadcasted_iota(jnp.int32, (BLK, BLK), 1)

    def body(it, state):
        carry, acc = state
        kb = qi - it
        start = pl.multiple_of(kb * BLK, BLK)
        k = k_ref[0, pl.ds(start, BLK), :]
        v = v_ref[0, pl.ds(start, BLK), :]
        kpos = start + cols
        mask = jnp.logical_and(kpos < qi * BLK + rows, kpos >= SEQ_OFF)
        return _sb_block(_dot(q, k, NT), bias, mask, u_ext, v, carry, acc)

    zero = jnp.zeros((BLK, BLK), F32)
    _, acc = lax.fori_loop(0, qi + 1, body, (zero, zero))
    o_ref[0] = acc * _silu(bz_ref[0])


def _sb_prompt(sb_bias, q, k, v, bz, u_ext):
    bsz, lp, _ = q.shape
    nq = lp // BLK
    qblk = pl.BlockSpec((1, BLK, B_DIM), lambda b, h, i: (b, i, h))
    kvblk = pl.BlockSpec((1, lp, B_DIM), lambda b, h, i: (b, 0, h))
    return pl.pallas_call(
        _sb_prompt_kernel,
        grid=(bsz, B_HEADS, nq),
        in_specs=[pl.BlockSpec(memory_space=pltpu.SMEM), qblk, kvblk, kvblk, qblk, _resident(u_ext.shape)],
        out_specs=qblk,
        out_shape=jax.ShapeDtypeStruct((bsz, lp, B_WIDTH), F32),
        compiler_params=_params("parallel", "parallel", "arbitrary"),
        name="sb_prompt",
    )(sb_bias, q, k, v, bz, u_ext)


def _sb_sample_kernel(pt_ref, bias_ref, q_ref, kn_ref, vn_ref, bz_ref, u_ref, *rest):
    npg = PAGES_PER_STEP
    k_refs, v_refs = rest[:npg], rest[npg:2 * npg]
    o_ref, carry_ref, acc_ref = rest[2 * npg:]
    c = pl.program_id(1)
    S = q_ref.shape[1]
    u_ext = u_ref[...]

    @pl.when(c == 0)
    def _():
        rows = lax.broadcasted_iota(jnp.int32, (S, BLK), 0)
        cols = lax.broadcasted_iota(jnp.int32, (S, BLK), 1)
        mask = cols < rows
        pad = jnp.zeros((BLK - S, B_WIDTH), BF16)
        kn = jnp.concatenate([kn_ref[0], pad], axis=0)
        vn = jnp.concatenate([vn_ref[0], pad], axis=0)
        for h in range(B_HEADS):
            sl = slice(h * B_DIM, (h + 1) * B_DIM)
            zero = jnp.zeros((S, BLK), F32)
            carry, acc = _sb_block(_dot(q_ref[0, :, sl], kn[:, sl], NT), bias_ref[h], mask, u_ext, vn[:, sl],
                                   zero, zero)
            carry_ref[h] = carry
            acc_ref[h] = acc

    for h in range(B_HEADS):
        q = q_ref[0, :, h * B_DIM:(h + 1) * B_DIM]
        carry, acc = carry_ref[h], acc_ref[h]
        for p in reversed(range(npg)):
            k = k_refs[p][0, :, h, :].astype(BF16)
            v = v_refs[p][0, :, h, :].astype(BF16)
            carry, acc = _sb_block(_dot(q, k, NT), bias_ref[h], None, u_ext, v, carry, acc)
        carry_ref[h] = carry
        acc_ref[h] = acc

    @pl.when(c == pl.num_programs(1) - 1)
    def _():
        for h in range(B_HEADS):
            sl = slice(h * B_DIM, (h + 1) * B_DIM)
            o_ref[0, :, sl] = acc_ref[h] * _silu(bz_ref[0, :, sl])


def _sb_sample(page_table, sb_bias, q, kn, vn, bz, u_ext, cache_k, cache_v):
    db, S, _ = q.shape
    n_pages = page_table.shape[1]
    npg = PAGES_PER_STEP
    assert n_pages % npg == 0
    nch = n_pages // npg
    row = pl.BlockSpec((1, S, B_WIDTH), lambda b, c, pt: (b, 0, 0))

    def page_spec(p):
        return pl.BlockSpec((1, PAGE_SIZE, B_HEADS, B_DIM),
                            lambda b, c, pt: (pt[b, (nch - 1 - c) * npg + p], 0, 0, 0))

    grid_spec = pltpu.PrefetchScalarGridSpec(
        num_scalar_prefetch=1,
        grid=(db, nch),
        in_specs=[pl.BlockSpec(memory_space=pltpu.SMEM), row, row, row, row,
                  pl.BlockSpec(u_ext.shape, lambda b, c, pt: (0, 0))]
                 + [page_spec(p) for p in range(npg)] * 2,
        out_specs=row,
        scratch_shapes=[pltpu.VMEM((B_HEADS, S, BLK), F32), pltpu.VMEM((B_HEADS, S, B_DIM), F32)],
    )
    return pl.pallas_call(
        _sb_sample_kernel,
        grid_spec=grid_spec,
        out_shape=jax.ShapeDtypeStruct((db, S, B_WIDTH), F32),
        compiler_params=_params("parallel", "arbitrary"),
        name="sb_sample",
    )(page_table, sb_bias, q, kn, vn, bz, u_ext, *([cache_k] * npg), *([cache_v] * npg))


def _seg_mean_sq(x, seg_ones):
    hi, lo = _split_bf16(x * x)
    w = seg_ones.shape[0]
    parts = []
    for c0 in range(0, x.shape[1], w):
        parts.append(_dot(hi[:, c0:c0 + w], seg_ones) + _dot(lo[:, c0:c0 + w], seg_ones))
    ss = parts[0] if len(parts) == 1 else jnp.concatenate(parts, axis=1)
    return ss * (1.0 / C_DIM)


def _mid_kernel(h_ref, oa_ref, ob_ref, woa_ref, wob_ref, g_ref, w_ref, qn_ref, kn_ref, seg_ref,
                h1_ref, q_ref, k_ref, v_ref, gate_ref, kh_ref, vh_ref):
    h1 = h_ref[...] + _dot(oa_ref[...].astype(BF16), woa_ref[...]) + _dot(ob_ref[...].astype(BF16), wob_ref[...])
    h1_ref[...] = h1
    xn = h1 * lax.rsqrt(jnp.mean(h1 * h1, axis=-1, keepdims=True) + EPS) * g_ref[...]
    xb = xn.astype(BF16)
    seg = seg_ref[...]
    q = _dot(xb, w_ref[:, 0:C_WIDTH])
    q = q * lax.rsqrt(_seg_mean_sq(q, seg) + EPS) * qn_ref[...]
    q_ref[...] = q.astype(BF16)
    k = _dot(xb, w_ref[:, C_WIDTH:C_WIDTH + C_KV_WIDTH])
    k = k * lax.rsqrt(_seg_mean_sq(k, seg) + EPS) * kn_ref[...]
    k_ref[...] = k
    kh_ref[...] = k.astype(BF16)
    v = _dot(xb, w_ref[:, C_WIDTH + C_KV_WIDTH:C_WIDTH + 2 * C_KV_WIDTH])
    v_ref[...] = v
    vh_ref[...] = v.astype(BF16)
    gate_ref[...] = _silu(_dot(xb, w_ref[:, C_WIDTH + 2 * C_KV_WIDTH:2 * C_WIDTH + 2 * C_KV_WIDTH]))


def _mid(h, oa, ob, woa, wob, g, w1, qn_row, kn_row, seg, tr):
    n = h.shape[0]
    sds = jax.ShapeDtypeStruct
    return pl.pallas_call(
        _mid_kernel,
        grid=(n // tr,),
        in_specs=[_rows(tr, D_MODEL), _rows(tr, A_WIDTH), _rows(tr, B_WIDTH), _resident(woa.shape),
                  _resident(wob.shape), _resident((1, D_MODEL)), _resident(w1.shape), _resident((1, C_WIDTH)),
                  _resident((1, C_KV_WIDTH)), _resident(seg.shape)],
        out_specs=[_rows(tr, D_MODEL), _rows(tr, C_WIDTH), _rows(tr, C_KV_WIDTH), _rows(tr, C_KV_WIDTH),
                   _rows(tr, C_WIDTH), _rows(tr, C_KV_WIDTH), _rows(tr, C_KV_WIDTH)],
        out_shape=[sds((n, D_MODEL), F32), sds((n, C_WIDTH), BF16), sds((n, C_KV_WIDTH), F32),
                   sds((n, C_KV_WIDTH), F32), sds((n, C_WIDTH), F32), sds((n, C_KV_WIDTH), BF16),
                   sds((n, C_KV_WIDTH), BF16)],
        compiler_params=_params("parallel"),
        name="mid",
    )(h, oa, ob, woa, wob, g, w1, qn_row, kn_row, seg)


def _swa_head(q, pieces, sink):
    scores = [_dot(q, k, NT) * (C_DIM ** -0.5) + b for k, _, b in pieces]
    mx = sink
    for s in scores:
        mx = jnp.maximum(mx, jnp.max(s, axis=-1, keepdims=True))
    denom = jnp.exp(sink - mx)
    o = None
    for s, (_, v, _) in zip(scores, pieces):
        p = jnp.exp(s - mx)
        denom = denom + jnp.sum(p, axis=-1, keepdims=True)
        t = _dot(p.astype(BF16), v)
        o = t if o is None else o + t
    return o / denom


def _swa_prompt_kernel(sink_ref, mbias_ref, q_ref, kp_ref, ko_ref, k0_ref, vp_ref, vo_ref, v0_ref, gate_ref,
                       tbl_ref, o_ref):
    qi = pl.program_id(1)
    col = lax.broadcasted_iota(jnp.int32, (1, BLK), 1)
    own_ok = col >= jnp.where(qi == 0, SEQ_OFF, 0)
    prev_ok = col >= jnp.where(qi == 0, BLK, jnp.where(qi == 1, SEQ_OFF, 0))
    row = lax.broadcasted_iota(jnp.int32, (BLK, N_META), 0)
    mcol = lax.broadcasted_iota(jnp.int32, (BLK, N_META), 1)
    meta_ok = qi * BLK + row - (SEQ_OFF + mcol) >= WINDOW
    for h in range(C_HEADS):
        g = h // (C_HEADS // C_KV_HEADS)
        sl = slice(h * C_DIM, (h + 1) * C_DIM)
        kv = slice(g * C_DIM, (g + 1) * C_DIM)
        pieces = [
            (kp_ref[0, :, kv], vp_ref[0, :, kv], jnp.where(prev_ok, tbl_ref[h, :, 0:BLK], NEG)),
            (ko_ref[0, :, kv], vo_ref[0, :, kv], jnp.where(own_ok, tbl_ref[h, :, BLK:2 * BLK], NEG)),
            (k0_ref[0, SEQ_OFF:SEQ_OFF + N_META, kv], v0_ref[0, SEQ_OFF:SEQ_OFF + N_META, kv],
             jnp.where(meta_ok, mbias_ref[h], NEG)),
        ]
        o = _swa_head(q_ref[0, :, sl], pieces, sink_ref[h])
        o_ref[0, :, sl] = (o * gate_ref[0, :, sl]).astype(o_ref.dtype)


def _swa_prompt(sinks, meta_bias, q, k, v, gate, tbl):
    bsz, lp, _ = q.shape
    nq = lp // BLK
    qblk = pl.BlockSpec((1, BLK, C_WIDTH), lambda b, i: (b, i, 0))
    own = pl.BlockSpec((1, BLK, C_KV_WIDTH), lambda b, i: (b, i, 0))
    prev = pl.BlockSpec((1, BLK, C_KV_WIDTH), lambda b, i: (b, jnp.maximum(i - 1, 0), 0))
    first = pl.BlockSpec((1, BLK, C_KV_WIDTH), lambda b, i: (b, 0, 0))
    smem = pl.BlockSpec(memory_space=pltpu.SMEM)
    return pl.pallas_call(
        _swa_prompt_kernel,
        grid=(bsz, nq),
        in_specs=[smem, smem, qblk, prev, own, first, prev, own, first, qblk, _resident(tbl.shape)],
        out_specs=qblk,
        out_shape=jax.ShapeDtypeStruct((bsz, lp, C_WIDTH), BF16),
        compiler_params=_params("parallel", "parallel"),
        name="swa_prompt",
    )(sinks, meta_bias, q, k, k, k, v, v, v, gate, tbl)


def _swa_sample_kernel(sink_ref, q_ref, km_ref, kw_ref, kn_ref, vm_ref, vw_ref, vn_ref, gate_ref,
                       tm_ref, tw_ref, tn_ref, o_ref):
    for h in range(C_HEADS):
        g = h // (C_HEADS // C_KV_HEADS)
        sl = slice(h * C_DIM, (h + 1) * C_DIM)
        kv = slice(g * C_DIM, (g + 1) * C_DIM)
        pieces = [
            (km_ref[0, :, kv].astype(BF16), vm_ref[0, :, kv].astype(BF16), tm_ref[h]),
            (kw_ref[0, :, kv].astype(BF16), vw_ref[0, :, kv].astype(BF16), tw_ref[h]),
            (kn_ref[0, :, kv], vn_ref[0, :, kv], tn_ref[h]),
        ]
        o = _swa_head(q_ref[0, :, sl], pieces, sink_ref[h])
        o_ref[0, :, sl] = (o * gate_ref[0, :, sl]).astype(o_ref.dtype)


def _swa_sample(sinks, q, km, kw, kn, vm, vw, vn, gate, tm, tw, tn):
    db, S, _ = q.shape
    blk = lambda r, w: pl.BlockSpec((1, r, w), lambda b: (b, 0, 0))
    return pl.pallas_call(
        _swa_sample_kernel,
        grid=(db,),
        in_specs=[pl.BlockSpec(memory_space=pltpu.SMEM), blk(S, C_WIDTH),
                  blk(N_META, C_KV_WIDTH), blk(WINDOW, C_KV_WIDTH), blk(S, C_KV_WIDTH),
                  blk(N_META, C_KV_WIDTH), blk(WINDOW, C_KV_WIDTH), blk(S, C_KV_WIDTH),
                  blk(S, C_WIDTH), _resident(tm.shape), _resident(tw.shape), _resident(tn.shape)],
        out_specs=blk(S, C_WIDTH),
        out_shape=jax.ShapeDtypeStruct((db, S, C_WIDTH), BF16),
        compiler_params=_params("parallel"),
        name="swa_sample",
    )(sinks, q, km, kw, kn, vm, vw, vn, gate, tm, tw, tn)


def _outproj_kernel(h_ref, o_ref, w_ref, y_ref):
    y_ref[...] = h_ref[...] + _dot(o_ref[...], w_ref[...])


def _outproj(h, o, w, tr):
    n = h.shape[0]
    return pl.pallas_call(
        _outproj_kernel,
        grid=(n // tr,),
        in_specs=[_rows(tr, D_MODEL), _rows(tr, C_WIDTH), _resident(w.shape)],
        out_specs=_rows(tr, D_MODEL),
        out_shape=jax.ShapeDtypeStruct((n, D_MODEL), F32),
        compiler_params=_params("parallel"),
        name="outproj1",
    )(h, o, w)


def _t5_bucket(d):
    d = jnp.maximum(d, 0)
    exact = N_BUCKETS // 2
    large = exact + (jnp.log(jnp.maximum(d, 1).astype(F32) / exact)
                     / math.log(MAX_DISTANCE / exact) * (N_BUCKETS - exact)).astype(jnp.int32)
    return jnp.where(d < exact, d, jnp.minimum(large, N_BUCKETS - 1))


def _bias_table(rel_bias, d, allow):
    b = jnp.moveaxis(rel_bias[_t5_bucket(d)], -1, 0).astype(F32)
    return jnp.where(allow[None], b, NEG)


def _row_tile(n, cap):
    t = cap
    while n % t:
        t //= 2
    return t


def kernel(x_prompt, x_sample, cache_sb_k, cache_sb_v, page_table, state_gdn, state_gdn_conv, cache_swa_k,
           cache_swa_v, cache_swa_meta_k, cache_swa_meta_v, meta_tokens, rel_bias, norm0, w_in0, conv0, a_log0,
           dt_bias0, gdn_norm0, sb_bias0, w_out0, norm1, w_in1, q_norm1, k_norm1, sinks1, w_out1):
    bsz, n_real, _ = x_prompt.shape
    db, dec_seq, _ = x_sample.shape
    past_len = page_table.shape[1] * PAGE_SIZE
    assert n_real % BLK == 0 and BLK % A_CHUNK == 0
    seq_end = SEQ_OFF + N_META + n_real
    lp = seq_end + A_CHUNK
    assert lp % BLK == 0

    a_end = A_QKV + A_WIDTH
    w0_main = jnp.concatenate([w_in0[:, :a_end], w_in0[:, a_end + 2 * A_HEADS:]], axis=1).astype(BF16)
    w0_ba = jnp.pad(w_in0[:, a_end:a_end + 2 * A_HEADS], ((0, 0), (0, LANES - 2 * A_HEADS)))
    lane_row = lambda vec, off: jnp.zeros((1, LANES), F32).at[0, off:off + vec.shape[0]].set(vec.astype(F32))
    alog_row = lane_row(a_log0, A_HEADS)
    dtb_row = lane_row(dt_bias0, A_HEADS)
    onorm_row = gdn_norm0.astype(F32)[None]
    g0 = norm0.astype(F32)[None]
    g1 = norm1.astype(F32)[None]
    woa = w_out0[:A_WIDTH].astype(BF16)
    wob = w_out0[A_WIDTH:].astype(BF16)
    w1 = w_in1.astype(BF16)
    wo1 = w_out1.astype(BF16)
    qn_row = jnp.tile(q_norm1.astype(F32), C_HEADS)[None]
    kn_row = jnp.tile(k_norm1.astype(F32), C_KV_HEADS)[None]
    seg_w = 2 * LANES
    seg = (jnp.arange(seg_w)[:, None] // C_DIM == jnp.arange(seg_w)[None, :] // C_DIM).astype(BF16)
    u_ext = _suffix_matrix()
    sb_bias = sb_bias0.astype(F32)
    sinks = sinks1.astype(F32)
    convw = conv0.astype(F32)

    i = jnp.arange(BLK)[:, None]
    c = jnp.arange(2 * BLK)[None, :]
    d_win = i + BLK - c
    tbl_p = _bias_table(rel_bias, d_win, (d_win >= 0) & (d_win < WINDOW))
    meta_bias = rel_bias[_t5_bucket(jnp.int32(WINDOW))].astype(F32)
    t_pos = past_len + jnp.arange(dec_seq)[:, None]
    s_meta = jnp.arange(N_META)[None, :]
    s_win = past_len - WINDOW + jnp.arange(WINDOW)[None, :]
    s_new = past_len + jnp.arange(dec_seq)[None, :]
    tbl_sm = _bias_table(rel_bias, t_pos - s_meta, (t_pos - s_meta) >= WINDOW)
    in_win = lambda d: (d >= 0) & (d < WINDOW)
    tbl_sw = _bias_table(rel_bias, t_pos - s_win, in_win(t_pos - s_win))
    tbl_sn = _bias_table(rel_bias, t_pos - s_new, in_win(t_pos - s_new))

    hp = jnp.concatenate([jnp.zeros((bsz, SEQ_OFF, D_MODEL), F32),
                          jnp.broadcast_to(meta_tokens.astype(F32)[None], (bsz, N_META, D_MODEL)),
                          x_prompt, jnp.zeros((bsz, lp - seq_end, D_MODEL), F32)], axis=1)
    hp = hp.reshape(bsz * lp, D_MODEL)
    hs = x_sample.reshape(db * dec_seq, D_MODEL)
    trp = _row_tile(bsz * lp, 256)
    trs = _row_tile(db * dec_seq, 256)

    qkv_p, z_p, bq_p, bk_p, bv_p, bz_p, ba_p, bkh_p, bvh_p = _inproj0(hp, g0, w0_main, w0_ba, trp)
    qkv_s, z_s, bq_s, bk_s, bv_s, bz_s, ba_s, bkh_s, bvh_s = _inproj0(hs, g0, w0_main, w0_ba, trs)
    seq = lambda a: a.reshape(bsz, lp, a.shape[-1])
    dec = lambda a: a.reshape(db, dec_seq, a.shape[-1])

    oa_p, gdn_p = _gdn_prompt(seq(qkv_p), seq(z_p), seq(ba_p), convw, alog_row, dtb_row, onorm_row)
    oa_s, gdn_s = _gdn_sample(dec(qkv_s), state_gdn_conv.astype(F32), dec(z_s), dec(ba_s), state_gdn.astype(F32),
                              convw, alog_row, dtb_row, onorm_row)
    ob_p = _sb_prompt(sb_bias, seq(bq_p), seq(bkh_p), seq(bvh_p), seq(bz_p), u_ext)
    ob_s = _sb_sample(page_table, sb_bias, dec(bq_s), dec(bkh_s), dec(bvh_s), dec(bz_s), u_ext,
                      cache_sb_k, cache_sb_v)

    h1_p, q1_p, k1_p, v1_p, gate_p, k1h_p, v1h_p = _mid(hp, oa_p.reshape(bsz * lp, A_WIDTH),
                                                         ob_p.reshape(bsz * lp, B_WIDTH), woa, wob, g1, w1,
                                                         qn_row, kn_row, seg, trp)
    h1_s, q1_s, k1_s, v1_s, gate_s, k1h_s, v1h_s = _mid(hs, oa_s.reshape(db * dec_seq, A_WIDTH),
                                                         ob_s.reshape(db * dec_seq, B_WIDTH), woa, wob, g1, w1,
                                                         qn_row, kn_row, seg, trs)

    oc_p = _swa_prompt(sinks, meta_bias, seq(q1_p), seq(k1h_p), seq(v1h_p), seq(gate_p), tbl_p)
    flat_kv = lambda a: a.reshape(a.shape[0], a.shape[1], C_KV_WIDTH)
    oc_s = _swa_sample(sinks, dec(q1_s), flat_kv(cache_swa_meta_k), flat_kv(cache_swa_k), dec(k1h_s),
                       flat_kv(cache_swa_meta_v), flat_kv(cache_swa_v), dec(v1h_s), dec(gate_s),
                       tbl_sm, tbl_sw, tbl_sn)
    y_p = _outproj(h1_p, oc_p.reshape(bsz * lp, C_WIDTH), wo1, trp)
    y_s = _outproj(h1_s, oc_s.reshape(db * dec_seq, C_WIDTH), wo1, trs)

    first = SEQ_OFF
    real0 = SEQ_OFF + N_META
    y_prompt = seq(y_p)[:, real0:seq_end]
    y_sample = dec(y_s)
    heads_b = lambda a: a.reshape(a.shape[0], a.shape[1], B_HEADS, B_DIM)
    heads_c = lambda a: a.reshape(a.shape[0], a.shape[1], C_KV_HEADS, C_DIM)
    sbk_p = heads_b(seq(bk_p)[:, first:seq_end])
    sbv_p = heads_b(seq(bv_p)[:, first:seq_end])
    sbk_s = heads_b(dec(bk_s))
    sbv_s = heads_b(dec(bv_s))
    conv_p = seq(qkv_p)[:, seq_end - (A_CONV - 1):seq_end]
    conv_s = jnp.concatenate([state_gdn_conv.astype(F32), dec(qkv_s)], axis=1)[:, -(A_CONV - 1):]
    k1_seq, v1_seq = seq(k1_p), seq(v1_p)
    swk_p = heads_c(k1_seq[:, seq_end - WINDOW:seq_end])
    swv_p = heads_c(v1_seq[:, seq_end - WINDOW:seq_end])
    swmk_p = heads_c(k1_seq[:, first:first + N_META])
    swmv_p = heads_c(v1_seq[:, first:first + N_META])
    swk_s = jnp.concatenate([cache_swa_k.astype(F32), heads_c(dec(k1_s))], axis=1)[:, -WINDOW:]
    swv_s = jnp.concatenate([cache_swa_v.astype(F32), heads_c(dec(v1_s))], axis=1)[:, -WINDOW:]
    return (y_prompt, y_sample, sbk_p, sbv_p, sbk_s, sbv_s, gdn_p, gdn_s, conv_p, conv_s,
            swk_p, swv_p, swmk_p, swmv_p, swk_s, swv_s)
```

```python
import functools
import math

import jax
import jax.numpy as jnp
import numpy as np
from jax import lax
from jax.experimental import pallas as pl
from jax.experimental.pallas import tpu as pltpu

F32 = jnp.float32
BF16 = jnp.bfloat16
HIGHEST = lax.Precision.HIGHEST

D_MODEL = 1024
N_META = 16
EPS = 1e-6
A_HEADS = 4
A_DK = 128
A_DV = 128
A_CONV = 4
A_CHUNK = 64
A_WIDTH = A_HEADS * A_DV
A_QKV = 2 * A_HEADS * A_DK + A_HEADS * A_DV
B_HEADS = 4
B_DIM = 128
B_WIDTH = B_HEADS * B_DIM
C_HEADS = 16
C_KV_HEADS = 4
C_DIM = 64
C_WIDTH = C_HEADS * C_DIM
C_KV_WIDTH = C_KV_HEADS * C_DIM
WINDOW = 128
N_BUCKETS = 32
MAX_DISTANCE = 128
PAGE_SIZE = 128

LANES = 128
BLK = 128
SB_BLK = 256
SEQ_OFF = A_CHUNK - N_META
NEG = -1e30
PAGES_PER_STEP = 8
PAGE_ROWS = PAGE_SIZE * B_HEADS
GDN_SEQS_PER_STEP = 2
VMEM_LIMIT = 56 * 1024 * 1024

NN = (((1,), (0,)), ((), ()))
NT = (((1,), (1,)), ((), ()))
TN = (((0,), (0,)), ((), ()))


def _dot(a, b, dims=NN):
    return lax.dot_general(a, b, dims, preferred_element_type=F32)


def _dot_hi(a, b, dims=NN):
    return lax.dot_general(a, b, dims, precision=HIGHEST, preferred_element_type=F32)


def _silu(x):
    return x * (1.0 / (1.0 + jnp.exp(-x)))


def _softplus(x):
    return jnp.maximum(x, 0.0) + jnp.log(1.0 + jnp.exp(-jnp.abs(x)))


def _split_bf16(x):
    hi = x.astype(BF16)
    lo = (x - hi.astype(F32)).astype(BF16)
    return hi, lo


def _dot3(a, b, dims=NN):
    ah, al = _split_bf16(a)
    bh, bl = _split_bf16(b)
    return _dot(ah, bh, dims) + (_dot(ah, bl, dims) + _dot(al, bh, dims))


def _dot_sel(sel, x):
    s = sel.astype(BF16)
    h1 = x.astype(BF16)
    r1 = x - h1.astype(F32)
    h2 = r1.astype(BF16)
    h3 = (r1 - h2.astype(F32)).astype(BF16)
    return _dot(s, h1) + (_dot(s, h2) + _dot(s, h3))


def _params(*sem):
    return pltpu.CompilerParams(dimension_semantics=sem, vmem_limit_bytes=VMEM_LIMIT)


def _rows(tr, width, *, squeeze=False):
    return pl.BlockSpec((tr, width), lambda i: (i, 0))


def _resident(shape):
    nd = len(shape)
    return pl.BlockSpec(shape, lambda *_: (0,) * nd)


_L0_COLS = (("qkv", A_QKV), ("z", A_WIDTH), ("bq", B_WIDTH), ("bk", B_WIDTH), ("bv", B_WIDTH),
            ("bz", B_WIDTH))


def _inproj0_kernel(x_ref, g_ref, w_ref, wba_ref, qkv_ref, z_ref, bq_ref, bk_ref, bv_ref, bz_ref,
                    ba_ref, bkh_ref, bvh_ref):
    x = x_ref[...]
    xn = x * lax.rsqrt(jnp.mean(x * x, axis=-1, keepdims=True) + EPS) * g_ref[...]
    xb = xn.astype(BF16)
    off = 0
    outs = {}
    for (name, width), ref in zip(_L0_COLS, (qkv_ref, z_ref, bq_ref, bk_ref, bv_ref, bz_ref)):
        u = _dot(xb, w_ref[:, off:off + width])
        outs[name] = u
        if name == "bq":
            u = u * (B_DIM ** -0.5)
        ref[...] = u.astype(ref.dtype)
        off += width
    bkh_ref[...] = outs["bk"].astype(BF16)
    bvh_ref[...] = outs["bv"].astype(BF16)
    ba_ref[...] = _dot_hi(xn, wba_ref[...])


def _inproj0(x, g, w_main, w_ba, tr):
    n = x.shape[0]
    widths = [w for _, w in _L0_COLS]
    out_shape = [jax.ShapeDtypeStruct((n, A_QKV), F32), jax.ShapeDtypeStruct((n, A_WIDTH), F32),
                 jax.ShapeDtypeStruct((n, B_WIDTH), BF16), jax.ShapeDtypeStruct((n, B_WIDTH), F32),
                 jax.ShapeDtypeStruct((n, B_WIDTH), F32), jax.ShapeDtypeStruct((n, B_WIDTH), F32),
                 jax.ShapeDtypeStruct((n, LANES), F32),
                 jax.ShapeDtypeStruct((n, B_WIDTH), BF16), jax.ShapeDtypeStruct((n, B_WIDTH), BF16)]
    out_specs = [_rows(tr, w) for w in widths] + [_rows(tr, LANES), _rows(tr, B_WIDTH), _rows(tr, B_WIDTH)]
    return pl.pallas_call(
        _inproj0_kernel,
        grid=(n // tr,),
        in_specs=[_rows(tr, D_MODEL), _resident((1, D_MODEL)), _resident(w_main.shape), _resident(w_ba.shape)],
        out_specs=out_specs,
        out_shape=out_shape,
        compiler_params=_params("parallel"),
        name="inproj0",
    )(x, g, w_main, w_ba)


def _gdn_chunk(c, ba, z, alog_row, dtb_row, onorm_row, s_list):
    return _gdn_chunks([(c, ba, z, s_list)], alog_row, dtb_row, onorm_row)[0]


def _gdn_chunks(items, alog_row, dtb_row, onorm_row):
    C = items[0][0].shape[0]
    ri = lax.broadcasted_iota(jnp.int32, (C, C), 0)
    ci = lax.broadcasted_iota(jnp.int32, (C, C), 1)
    incl = ri >= ci
    strict = ri > ci
    eye = ri == ci
    eye_f = eye.astype(F32)
    ones = jnp.ones((C, C), F32)
    ch = []
    for c, ba, z, s_list in items:
        beta_all = 1.0 / (1.0 + jnp.exp(-ba))
        g_all = -jnp.exp(alog_row) * _softplus(ba + dtb_row)
        gc_all = _dot_sel(incl, g_all)
        gl_all = gc_all[C - 1:C, :]
        for h in range(A_HEADS):
            q = c[:, h * A_DK:(h + 1) * A_DK]
            k = c[:, (A_HEADS + h) * A_DK:(A_HEADS + h + 1) * A_DK]
            ch.append(dict(
                q=q * lax.rsqrt(jnp.sum(q * q, axis=-1, keepdims=True) + EPS) * (A_DK ** -0.5),
                k=k * lax.rsqrt(jnp.sum(k * k, axis=-1, keepdims=True) + EPS),
                v=c[:, 2 * A_HEADS * A_DK + h * A_DV:2 * A_HEADS * A_DK + (h + 1) * A_DV],
                beta=beta_all[:, h:h + 1], gc=gc_all[:, A_HEADS + h:A_HEADS + h + 1],
                gl=gl_all[:, A_HEADS + h:A_HEADS + h + 1], s=s_list[h], z=z[:, h * A_DV:(h + 1) * A_DV]))
    for x in ch:
        gcb = jnp.broadcast_to(x["gc"], (C, C))
        gcr = _dot_sel(ones, jnp.where(eye, gcb, 0.0))
        x["decay"] = jnp.exp(jnp.where(incl, gcb - gcr, NEG))
        x["kb"] = x["k"] * x["beta"]
    for x in ch:
        x["p"] = -jnp.where(strict, _dot3(x["kb"], x["k"], NT) * x["decay"], 0.0)
        x["t"] = eye_f + x["p"]
    n = 2
    while n < C:
        for x in ch:
            x["p"] = _dot3(x["p"], x["p"])
        for x in ch:
            x["t"] = x["t"] + _dot3(x["t"], x["p"])
        n *= 2
    for x in ch:
        x["eg"] = jnp.exp(x["gc"])
        x["u"] = _dot3(x["t"], x["v"] * x["beta"])
        x["w"] = _dot3(x["t"], x["kb"] * x["eg"])
        x["a_qk"] = _dot3(x["q"], x["k"], NT) * x["decay"]
    for x in ch:
        x["v_new"] = x["u"] - _dot3(x["w"], x["s"])
    for x in ch:
        kd = x["k"] * jnp.exp(x["gl"] - x["gc"])
        x["s_new"] = x["s"] * jnp.exp(x["gl"]) + _dot3(kd, x["v_new"], TN)
    for x in ch:
        o = _dot3(x["q"] * x["eg"], x["s"]) + _dot3(x["a_qk"], x["v_new"])
        o = o * lax.rsqrt(jnp.mean(o * o, axis=-1, keepdims=True) + EPS) * onorm_row
        x["o"] = o * _silu(x["z"])
    results = []
    for i in range(len(items)):
        mine = ch[i * A_HEADS:(i + 1) * A_HEADS]
        results.append(([x["o"] for x in mine], [x["s_new"] for x in mine]))
    return results


def _conv_silu(xbuf_ref, base, rows, convw_ref):
    acc = None
    for j in range(A_CONV):
        term = xbuf_ref[pl.ds(base - (A_CONV - 1) + j, rows), :] * convw_ref[j:j + 1, :]
        acc = term if acc is None else acc + term
    return _silu(acc)


_XPAD = 8


def _gdn_prompt_kernel(n_real_blocks, qkv_ref, z_ref, ba_ref, convw_ref, alog_ref, dtb_ref, onorm_ref,
                       o_ref, s_out_ref, xbuf_ref, s_ref):
    n = pl.program_id(0)
    bsz = qkv_ref.shape[0]
    C = A_CHUNK

    @pl.when(n == 0)
    def _():
        xbuf_ref[:, 0:_XPAD, :] = jnp.zeros((bsz, _XPAD, A_QKV), F32)
        s_ref[...] = jnp.zeros_like(s_ref)

    xbuf_ref[:, _XPAD:_XPAD + C, :] = qkv_ref[...]

    def run(base, rows):
        items = [(_conv_silu(xbuf_ref.at[b], _XPAD + base, rows, convw_ref), ba_ref[b, base:base + rows, :],
                  z_ref[b, base:base + rows, :], [s_ref[b, h] for h in range(A_HEADS)]) for b in range(bsz)]
        results = _gdn_chunks(items, alog_ref[...], dtb_ref[...], onorm_ref[...])
        for b, (outs, new_s) in enumerate(results):
            for h in range(A_HEADS):
                o_ref[b, base:base + rows, h * A_DV:(h + 1) * A_DV] = outs[h]
                s_ref[b, h] = new_s[h]

    @pl.when(n == 0)
    def _():
        o_ref[:, 0:SEQ_OFF, :] = jnp.zeros((bsz, SEQ_OFF, A_WIDTH), F32)
        run(SEQ_OFF, N_META)

    @pl.when(jnp.logical_and(n > 0, n <= n_real_blocks))
    def _():
        run(0, C)

    @pl.when(n > n_real_blocks)
    def _():
        o_ref[...] = jnp.zeros((bsz, C, A_WIDTH), F32)

    @pl.when(n == n_real_blocks)
    def _():
        s_out_ref[...] = s_ref[...]

    xbuf_ref[:, 0:_XPAD, :] = xbuf_ref[:, C:C + _XPAD, :]


def _gdn_prompt(qkv, z, ba, convw, alog_row, dtb_row, onorm_row, n_real):
    bsz, lp, _ = qkv.shape
    C = A_CHUNK
    nblk = lp // C
    blk = lambda w: pl.BlockSpec((bsz, C, w), lambda n: (0, n, 0))
    return pl.pallas_call(
        functools.partial(_gdn_prompt_kernel, n_real // C),
        grid=(nblk,),
        in_specs=[blk(A_QKV), blk(A_WIDTH), blk(LANES), _resident(convw.shape), _resident((1, LANES)),
                  _resident((1, LANES)), _resident((1, A_DV))],
        out_specs=[blk(A_WIDTH), _resident((bsz, A_HEADS, A_DK, A_DV))],
        out_shape=[jax.ShapeDtypeStruct((bsz, lp, A_WIDTH), F32),
                   jax.ShapeDtypeStruct((bsz, A_HEADS, A_DK, A_DV), F32)],
        scratch_shapes=[pltpu.VMEM((bsz, _XPAD + C, A_QKV), F32), pltpu.VMEM((bsz, A_HEADS, A_DK, A_DV), F32)],
        compiler_params=_params("arbitrary"),
        name="gdn_prompt",
    )(qkv, z, ba, convw, alog_row, dtb_row, onorm_row)


def _gdn_sample_kernel(qkv_ref, buf_ref, z_ref, ba_ref, s0_ref, convw_ref, alog_ref, dtb_ref, onorm_ref,
                       o_ref, s_out_ref, xbuf_ref):
    nb, S, _ = qkv_ref.shape
    xbuf_ref[:, 0:_XPAD, :] = jnp.zeros((nb, _XPAD, A_QKV), F32)
    xbuf_ref[:, _XPAD - (A_CONV - 1):_XPAD, :] = buf_ref[...]
    xbuf_ref[:, _XPAD:_XPAD + S, :] = qkv_ref[...]
    items = [(_conv_silu(xbuf_ref.at[b], _XPAD, S, convw_ref), ba_ref[b], z_ref[b],
              [s0_ref[b, h] for h in range(A_HEADS)]) for b in range(nb)]
    results = _gdn_chunks(items, alog_ref[...], dtb_ref[...], onorm_ref[...])
    for b, (outs, new_s) in enumerate(results):
        for h in range(A_HEADS):
            o_ref[b, :, h * A_DV:(h + 1) * A_DV] = outs[h]
            s_out_ref[b, h] = new_s[h]


def _gdn_sample(qkv, buf, z, ba, s0, convw, alog_row, dtb_row, onorm_row):
    db, S, _ = qkv.shape
    nb = GDN_SEQS_PER_STEP if db % GDN_SEQS_PER_STEP == 0 else 1
    blk = lambda r, w: pl.BlockSpec((nb, r, w), lambda b: (b, 0, 0))
    sblk = pl.BlockSpec((nb, A_HEADS, A_DK, A_DV), lambda b: (b, 0, 0, 0))
    return pl.pallas_call(
        _gdn_sample_kernel,
        grid=(db // nb,),
        in_specs=[blk(S, A_QKV), blk(A_CONV - 1, A_QKV), blk(S, A_WIDTH), blk(S, LANES), sblk,
                  _resident(convw.shape), _resident((1, LANES)), _resident((1, LANES)), _resident((1, A_DV))],
        out_specs=[blk(S, A_WIDTH), sblk],
        out_shape=[jax.ShapeDtypeStruct((db, S, A_WIDTH), F32),
                   jax.ShapeDtypeStruct((db, A_HEADS, A_DK, A_DV), F32)],
        scratch_shapes=[pltpu.VMEM((nb, _XPAD + S, A_QKV), F32)],
        compiler_params=_params("parallel"),
        name="gdn_sample",
    )(qkv, buf, z, ba, s0, convw, alog_row, dtb_row, onorm_row)


def _sb_tile(s, bias, mask, u_mat, v_blk, carry, acc):
    z = s + bias
    sp = _softplus(z)
    log_beta = z - sp
    if mask is not None:
        sp = jnp.where(mask, sp, 0.0)
    n = sp.shape[0]
    hi, lo = _split_bf16(sp)
    r2 = _dot(jnp.concatenate([hi, lo], axis=0), u_mat)
    w = jnp.exp(log_beta - (r2[:n] + r2[n:]) - carry)
    if mask is not None:
        w = jnp.where(mask, w, 0.0)
    acc = acc + _dot(w.astype(BF16), v_blk)
    carry = carry + jnp.sum(sp, axis=-1, keepdims=True)
    return carry, acc


def _sb_prompt_kernel(bias_ref, q_ref, k_ref, v_ref, bz_ref, u_ref, o_ref):
    qi = pl.program_id(1)
    T = SB_BLK
    u_mat = u_ref[...]
    rows = lax.broadcasted_iota(jnp.int32, (T, T), 0)
    cols = lax.broadcasted_iota(jnp.int32, (T, T), 1)
    causal = cols < rows

    def tile(kb, state, mask):
        start = pl.multiple_of(kb * T, T)
        heads = range(B_HEADS)
        sls = [pl.ds(h * B_DIM, B_DIM) for h in heads]
        z = [_dot(q_ref[0, :, sls[h]], k_ref[0, pl.ds(start, T), sls[h]], NT) + bias_ref[h] for h in heads]
        sp = [_softplus(z[h]) for h in heads]
        log_beta = [z[h] - sp[h] for h in heads]
        if mask is not None:
            sp = [jnp.where(mask, sp[h], 0.0) for h in heads]
        parts = []
        for h in heads:
            parts.extend(_split_bf16(sp[h]))
        r2 = _dot(jnp.concatenate(parts, axis=0), u_mat)
        new = []
        for h in heads:
            carry, acc = state[h]
            r = r2[2 * T * h:2 * T * h + T] + r2[2 * T * h + T:2 * T * (h + 1)]
            w = jnp.exp(log_beta[h] - r - carry)
            if mask is not None:
                w = jnp.where(mask, w, 0.0)
            acc = acc + _dot(w.astype(BF16), v_ref[0, pl.ds(start, T), sls[h]])
            new.append((carry + jnp.sum(sp[h], axis=-1, keepdims=True), acc))
        return tuple(new)

    state = tuple((jnp.zeros((T, 1), F32), jnp.zeros((T, B_DIM), F32)) for _ in range(B_HEADS))
    state = tile(qi, state, causal)
    state = lax.fori_loop(0, qi, lambda it, st: tile(qi - 1 - it, st, None), state)
    for h in range(B_HEADS):
        sl = pl.ds(h * B_DIM, B_DIM)
        o_ref[0, :, sl] = state[h][1] * _silu(bz_ref[0, :, sl])


def _sb_prompt(sb_bias, q, k, v, bz, u_mat):
    bsz, lp, _ = q.shape
    nq = lp // SB_BLK
    qblk = pl.BlockSpec((1, SB_BLK, B_WIDTH), lambda b, i: (b, i, 0))
    kvblk = pl.BlockSpec((1, lp, B_WIDTH), lambda b, i: (b, 0, 0))
    return pl.pallas_call(
        _sb_prompt_kernel,
        grid=(bsz, nq),
        in_specs=[pl.BlockSpec(memory_space=pltpu.SMEM), qblk, kvblk, kvblk, qblk, _resident(u_mat.shape)],
        out_specs=qblk,
        out_shape=jax.ShapeDtypeStruct((bsz, lp, B_WIDTH), F32),
        compiler_params=_params("parallel", "arbitrary"),
        name="sb_prompt",
    )(sb_bias, q, k, v, bz, u_mat)


def _sb_sample_kernel(pt_ref, bias_ref, q_ref, kn_ref, vn_ref, bz_ref, u_ref, *rest):
    npg = PAGES_PER_STEP
    k_refs, v_refs = rest[:npg], rest[npg:2 * npg]
    o_ref, carry_ref, acc_ref, kbuf_ref, vbuf_ref = rest[2 * npg:]
    c = pl.program_id(1)
    T = SB_BLK
    R = q_ref.shape[1]
    S = R // B_HEADS
    u_mat = u_ref[...]
    q = q_ref[0]
    rows = lax.broadcasted_iota(jnp.int32, (R, T), 0)
    cols = lax.broadcasted_iota(jnp.int32, (R, T), 1)
    row_head = rows // S
    same_head = (cols % B_HEADS) == row_head
    rid = lax.broadcasted_iota(jnp.int32, (R, 1), 0) // S
    bias = jnp.zeros((R, 1), F32)
    for h in range(B_HEADS):
        bias = jnp.where(rid == h, bias_ref[h], bias)

    @pl.when(c == 0)
    def _():
        new_mask = jnp.logical_and(same_head, cols // B_HEADS < rows % S)
        carry, acc = _sb_tile(_dot(q, kn_ref[0], NT), bias, new_mask, u_mat, vn_ref[0],
                              jnp.zeros((R, 1), F32), jnp.zeros((R, B_DIM), F32))
        carry_ref[...] = jnp.broadcast_to(carry, carry_ref.shape)
        acc_ref[...] = acc

    for p in range(npg):
        kbuf_ref[p * PAGE_ROWS:(p + 1) * PAGE_ROWS, :] = k_refs[p][0].astype(BF16)
        vbuf_ref[p * PAGE_ROWS:(p + 1) * PAGE_ROWS, :] = v_refs[p][0].astype(BF16)
    ncol = npg * PAGE_ROWS
    nt = ncol // T
    carry_in = carry_ref[:, 0:1]
    z = _dot(q, kbuf_ref[...], NT) + bias
    sp = _softplus(z)
    log_beta = z - sp
    match = (lax.broadcasted_iota(jnp.int32, (R, ncol), 1) % B_HEADS) == (
        lax.broadcasted_iota(jnp.int32, (R, ncol), 0) // S)
    sp = jnp.where(match, sp, 0.0)
    hi, lo = _split_bf16(sp)
    stacked = jnp.concatenate([part[:, t * T:(t + 1) * T] for t in range(nt) for part in (hi, lo)], axis=0)
    r2 = _dot(stacked, u_mat)
    later = carry_in
    ws = [None] * nt
    for t in reversed(range(nt)):
        cs = slice(t * T, (t + 1) * T)
        r = r2[2 * R * t:2 * R * t + R] + r2[2 * R * t + R:2 * R * (t + 1)]
        ws[t] = jnp.exp(log_beta[:, cs] - r - later)
        later = later + jnp.sum(sp[:, cs], axis=-1, keepdims=True)
    w = jnp.where(match, jnp.concatenate(ws, axis=1), 0.0)
    acc = acc_ref[...] + _dot(w.astype(BF16), vbuf_ref[...])
    carry_ref[...] = jnp.broadcast_to(later, carry_ref.shape)
    acc_ref[...] = acc

    @pl.when(c == pl.num_programs(1) - 1)
    def _():
        for h in range(B_HEADS):
            sl = slice(h * B_DIM, (h + 1) * B_DIM)
            o_ref[0, :, sl] = acc[h * S:(h + 1) * S, :] * _silu(bz_ref[0, :, sl])


def _sb_sample(page_table, sb_bias, q_rows, kn_rows, vn_rows, bz, u_mat, cache_k, cache_v):
    db, R, _ = q_rows.shape
    S = R // B_HEADS
    n_pages = page_table.shape[1]
    npg = PAGES_PER_STEP
    assert n_pages % npg == 0 and PAGE_ROWS % SB_BLK == 0
    nch = n_pages // npg
    per_seq = lambda r, w: pl.BlockSpec((1, r, w), lambda b, c, pt: (b, 0, 0))

    def page_spec(p):
        return pl.BlockSpec((1, PAGE_ROWS, B_DIM), lambda b, c, pt: (pt[b, (nch - 1 - c) * npg + p], 0, 0))

    grid_spec = pltpu.PrefetchScalarGridSpec(
        num_scalar_prefetch=1,
        grid=(db, nch),
        in_specs=[pl.BlockSpec(memory_space=pltpu.SMEM), per_seq(R, B_DIM), per_seq(SB_BLK, B_DIM),
                  per_seq(SB_BLK, B_DIM), per_seq(S, B_WIDTH), pl.BlockSpec(u_mat.shape, lambda b, c, pt: (0, 0))]
                 + [page_spec(p) for p in range(npg)] * 2,
        out_specs=per_seq(S, B_WIDTH),
        scratch_shapes=[pltpu.VMEM((R, LANES), F32), pltpu.VMEM((R, B_DIM), F32),
                        pltpu.VMEM((npg * PAGE_ROWS, B_DIM), BF16), pltpu.VMEM((npg * PAGE_ROWS, B_DIM), BF16)],
    )
    return pl.pallas_call(
        _sb_sample_kernel,
        grid_spec=grid_spec,
        out_shape=jax.ShapeDtypeStruct((db, S, B_WIDTH), F32),
        compiler_params=_params("parallel", "arbitrary"),
        name="sb_sample",
    )(page_table, sb_bias, q_rows, kn_rows, vn_rows, bz, u_mat, *([cache_k] * npg), *([cache_v] * npg))


def _seg_mean_sq(x, seg_ones):
    hi, lo = _split_bf16(x * x)
    w = seg_ones.shape[0]
    parts = []
    for c0 in range(0, x.shape[1], w):
        parts.append(_dot(hi[:, c0:c0 + w], seg_ones) + _dot(lo[:, c0:c0 + w], seg_ones))
    ss = parts[0] if len(parts) == 1 else jnp.concatenate(parts, axis=1)
    return ss * (1.0 / C_DIM)


def _mid_kernel(h_ref, oa_ref, ob_ref, woa_ref, wob_ref, g_ref, w_ref, qn_ref, kn_ref, seg_ref,
                h1_ref, q_ref, k_ref, v_ref, gate_ref, kh_ref, vh_ref):
    h1 = h_ref[...] + _dot(oa_ref[...].astype(BF16), woa_ref[...]) + _dot(ob_ref[...].astype(BF16), wob_ref[...])
    h1_ref[...] = h1
    xn = h1 * lax.rsqrt(jnp.mean(h1 * h1, axis=-1, keepdims=True) + EPS) * g_ref[...]
    xb = xn.astype(BF16)
    seg = seg_ref[...]
    q = _dot(xb, w_ref[:, 0:C_WIDTH])
    q = q * lax.rsqrt(_seg_mean_sq(q, seg) + EPS) * qn_ref[...]
    q_ref[...] = q.astype(BF16)
    k = _dot(xb, w_ref[:, C_WIDTH:C_WIDTH + C_KV_WIDTH])
    k = k * lax.rsqrt(_seg_mean_sq(k, seg) + EPS) * kn_ref[...]
    k_ref[...] = k
    kh_ref[...] = k.astype(BF16)
    v = _dot(xb, w_ref[:, C_WIDTH + C_KV_WIDTH:C_WIDTH + 2 * C_KV_WIDTH])
    v_ref[...] = v
    vh_ref[...] = v.astype(BF16)
    gate_ref[...] = _silu(_dot(xb, w_ref[:, C_WIDTH + 2 * C_KV_WIDTH:2 * C_WIDTH + 2 * C_KV_WIDTH]))


def _mid(h, oa, ob, woa, wob, g, w1, qn_row, kn_row, seg, tr):
    n = h.shape[0]
    sds = jax.ShapeDtypeStruct
    return pl.pallas_call(
        _mid_kernel,
        grid=(n // tr,),
        in_specs=[_rows(tr, D_MODEL), _rows(tr, A_WIDTH), _rows(tr, B_WIDTH), _resident(woa.shape),
                  _resident(wob.shape), _resident((1, D_MODEL)), _resident(w1.shape), _resident((1, C_WIDTH)),
                  _resident((1, C_KV_WIDTH)), _resident(seg.shape)],
        out_specs=[_rows(tr, D_MODEL), _rows(tr, C_WIDTH), _rows(tr, C_KV_WIDTH), _rows(tr, C_KV_WIDTH),
                   _rows(tr, C_WIDTH), _rows(tr, C_KV_WIDTH), _rows(tr, C_KV_WIDTH)],
        out_shape=[sds((n, D_MODEL), F32), sds((n, C_WIDTH), BF16), sds((n, C_KV_WIDTH), F32),
                   sds((n, C_KV_WIDTH), F32), sds((n, C_WIDTH), F32), sds((n, C_KV_WIDTH), BF16),
                   sds((n, C_KV_WIDTH), BF16)],
        compiler_params=_params("parallel"),
        name="mid",
    )(h, oa, ob, woa, wob, g, w1, qn_row, kn_row, seg)


def _swa_heads(qs, pieces_per_head, sinks):
    heads = range(len(qs))
    scores = [[_dot(qs[h], k, NT) * (C_DIM ** -0.5) + b for k, _, b in pieces_per_head[h]] for h in heads]
    mx = []
    for h in heads:
        m = sinks[h]
        for s in scores[h]:
            m = jnp.maximum(m, jnp.max(s, axis=-1, keepdims=True))
        mx.append(m)
    probs = [[jnp.exp(s - mx[h]) for s in scores[h]] for h in heads]
    outs = []
    for h in heads:
        denom = jnp.exp(sinks[h] - mx[h])
        o = None
        for p, (_, v, _) in zip(probs[h], pieces_per_head[h]):
            denom = denom + jnp.sum(p, axis=-1, keepdims=True)
            t = _dot(p.astype(BF16), v)
            o = t if o is None else o + t
        outs.append(o / denom)
    return outs


def _swa_prompt_kernel(sink_ref, mbias_ref, q_ref, kp_ref, ko_ref, k0_ref, vp_ref, vo_ref, v0_ref, gate_ref,
                       tbl_ref, o_ref):
    qi = pl.program_id(1)
    col = lax.broadcasted_iota(jnp.int32, (1, BLK), 1)
    own_ok = col >= jnp.where(qi == 0, SEQ_OFF, 0)
    prev_ok = col >= jnp.where(qi == 0, BLK, jnp.where(qi == 1, SEQ_OFF, 0))
    row = lax.broadcasted_iota(jnp.int32, (BLK, N_META), 0)
    mcol = lax.broadcasted_iota(jnp.int32, (BLK, N_META), 1)
    meta_ok = qi * BLK + row - (SEQ_OFF + mcol) >= WINDOW
    sls = [slice(h * C_DIM, (h + 1) * C_DIM) for h in range(C_HEADS)]
    pieces = []
    for h in range(C_HEADS):
        g = h // (C_HEADS // C_KV_HEADS)
        kv = slice(g * C_DIM, (g + 1) * C_DIM)
        pieces.append([
            (kp_ref[0, :, kv], vp_ref[0, :, kv], jnp.where(prev_ok, tbl_ref[h, :, 0:BLK], NEG)),
            (ko_ref[0, :, kv], vo_ref[0, :, kv], jnp.where(own_ok, tbl_ref[h, :, BLK:2 * BLK], NEG)),
            (k0_ref[0, SEQ_OFF:SEQ_OFF + N_META, kv], v0_ref[0, SEQ_OFF:SEQ_OFF + N_META, kv],
             jnp.where(meta_ok, mbias_ref[h], NEG)),
        ])
    outs = _swa_heads([q_ref[0, :, sl] for sl in sls], pieces, [sink_ref[h] for h in range(C_HEADS)])
    for h, sl in enumerate(sls):
        o_ref[0, :, sl] = (outs[h] * gate_ref[0, :, sl]).astype(o_ref.dtype)


def _swa_prompt(sinks, meta_bias, q, k, v, gate, tbl):
    bsz, lp, _ = q.shape
    nq = lp // BLK
    qblk = pl.BlockSpec((1, BLK, C_WIDTH), lambda b, i: (b, i, 0))
    own = pl.BlockSpec((1, BLK, C_KV_WIDTH), lambda b, i: (b, i, 0))
    prev = pl.BlockSpec((1, BLK, C_KV_WIDTH), lambda b, i: (b, jnp.maximum(i - 1, 0), 0))
    first = pl.BlockSpec((1, BLK, C_KV_WIDTH), lambda b, i: (b, 0, 0))
    smem = pl.BlockSpec(memory_space=pltpu.SMEM)
    return pl.pallas_call(
        _swa_prompt_kernel,
        grid=(bsz, nq),
        in_specs=[smem, smem, qblk, prev, own, first, prev, own, first, qblk, _resident(tbl.shape)],
        out_specs=qblk,
        out_shape=jax.ShapeDtypeStruct((bsz, lp, C_WIDTH), BF16),
        compiler_params=_params("parallel", "parallel"),
        name="swa_prompt",
    )(sinks, meta_bias, q, k, k, k, v, v, v, gate, tbl)


def _swa_sample_kernel(sink_ref, q_ref, km_ref, kw_ref, kn_ref, vm_ref, vw_ref, vn_ref, gate_ref,
                       tm_ref, tw_ref, tn_ref, o_ref):
    sls = [slice(h * C_DIM, (h + 1) * C_DIM) for h in range(C_HEADS)]
    kvs = [slice(g * C_DIM, (g + 1) * C_DIM) for g in range(C_KV_HEADS)]
    cached = [[(km_ref[0, :, kv].astype(BF16), vm_ref[0, :, kv].astype(BF16)),
               (kw_ref[0, :, kv].astype(BF16), vw_ref[0, :, kv].astype(BF16)),
               (kn_ref[0, :, kv], vn_ref[0, :, kv])] for kv in kvs]
    pieces = []
    for h in range(C_HEADS):
        g = h // (C_HEADS // C_KV_HEADS)
        pieces.append([(k, v, t[h]) for (k, v), t in zip(cached[g], (tm_ref, tw_ref, tn_ref))])
    outs = _swa_heads([q_ref[0, :, sl] for sl in sls], pieces, [sink_ref[h] for h in range(C_HEADS)])
    for h, sl in enumerate(sls):
        o_ref[0, :, sl] = (outs[h] * gate_ref[0, :, sl]).astype(o_ref.dtype)


def _swa_sample(sinks, q, km, kw, kn, vm, vw, vn, gate, tm, tw, tn):
    db, S, _ = q.shape
    blk = lambda r, w: pl.BlockSpec((1, r, w), lambda b: (b, 0, 0))
    return pl.pallas_call(
        _swa_sample_kernel,
        grid=(db,),
        in_specs=[pl.BlockSpec(memory_space=pltpu.SMEM), blk(S, C_WIDTH),
                  blk(N_META, C_KV_WIDTH), blk(WINDOW, C_KV_WIDTH), blk(S, C_KV_WIDTH),
                  blk(N_META, C_KV_WIDTH), blk(WINDOW, C_KV_WIDTH), blk(S, C_KV_WIDTH),
                  blk(S, C_WIDTH), _resident(tm.shape), _resident(tw.shape), _resident(tn.shape)],
        out_specs=blk(S, C_WIDTH),
        out_shape=jax.ShapeDtypeStruct((db, S, C_WIDTH), BF16),
        compiler_params=_params("parallel"),
        name="swa_sample",
    )(sinks, q, km, kw, kn, vm, vw, vn, gate, tm, tw, tn)


def _outproj_kernel(h_ref, o_ref, w_ref, y_ref):
    y_ref[...] = h_ref[...] + _dot(o_ref[...], w_ref[...])


def _outproj(h, o, w, tr):
    n = h.shape[0]
    return pl.pallas_call(
        _outproj_kernel,
        grid=(n // tr,),
        in_specs=[_rows(tr, D_MODEL), _rows(tr, C_WIDTH), _resident(w.shape)],
        out_specs=_rows(tr, D_MODEL),
        out_shape=jax.ShapeDtypeStruct((n, D_MODEL), F32),
        compiler_params=_params("parallel"),
        name="outproj1",
    )(h, o, w)


def _t5_bucket(d):
    d = np.maximum(np.asarray(d, np.int64), 0)
    exact = N_BUCKETS // 2
    large = exact + (np.log(np.maximum(d, 1) / exact) / math.log(MAX_DISTANCE / exact)
                     * (N_BUCKETS - exact)).astype(np.int64)
    return np.where(d < exact, d, np.minimum(large, N_BUCKETS - 1))


def _bias_table(rel_bias, d, allow):
    onehot = (_t5_bucket(d)[None] == np.arange(N_BUCKETS)[:, None, None]) & np.asarray(allow)[None]
    b = jnp.einsum("nh,nqk->hqk", rel_bias.astype(F32), jnp.asarray(onehot, F32), precision=HIGHEST)
    return jnp.where(jnp.asarray(np.asarray(allow))[None], b, NEG)


def _row_tile(n, cap):
    t = cap
    while n % t:
        t //= 2
    return t


def kernel(x_prompt, x_sample, cache_sb_k, cache_sb_v, page_table, state_gdn, state_gdn_conv, cache_swa_k,
           cache_swa_v, cache_swa_meta_k, cache_swa_meta_v, meta_tokens, rel_bias, norm0, w_in0, conv0, a_log0,
           dt_bias0, gdn_norm0, sb_bias0, w_out0, norm1, w_in1, q_norm1, k_norm1, sinks1, w_out1):
    bsz, n_real, _ = x_prompt.shape
    db, dec_seq, _ = x_sample.shape
    past_len = page_table.shape[1] * PAGE_SIZE
    assert n_real % SB_BLK == 0 and SB_BLK % BLK == 0 and BLK % A_CHUNK == 0
    seq_end = SEQ_OFF + N_META + n_real
    lp = n_real + SB_BLK

    a_end = A_QKV + A_WIDTH
    w0_main = jnp.concatenate([w_in0[:, :a_end], w_in0[:, a_end + 2 * A_HEADS:]], axis=1).astype(BF16)
    w0_ba = jnp.pad(w_in0[:, a_end:a_end + 2 * A_HEADS], ((0, 0), (0, LANES - 2 * A_HEADS)))
    lane_row = lambda vec, off: jnp.zeros((1, LANES), F32).at[0, off:off + vec.shape[0]].set(vec.astype(F32))
    alog_row = lane_row(a_log0, A_HEADS)
    dtb_row = lane_row(dt_bias0, A_HEADS)
    onorm_row = gdn_norm0.astype(F32)[None]
    g0 = norm0.astype(F32)[None]
    g1 = norm1.astype(F32)[None]
    woa = w_out0[:A_WIDTH].astype(BF16)
    wob = w_out0[A_WIDTH:].astype(BF16)
    w1 = w_in1.astype(BF16)
    wo1 = w_out1.astype(BF16)
    qn_row = jnp.tile(q_norm1.astype(F32), C_HEADS)[None]
    kn_row = jnp.tile(k_norm1.astype(F32), C_KV_HEADS)[None]
    seg_w = 2 * LANES
    seg = (jnp.arange(seg_w)[:, None] // C_DIM == jnp.arange(seg_w)[None, :] // C_DIM).astype(BF16)
    tj = jnp.arange(SB_BLK)[:, None]
    tc = jnp.arange(SB_BLK)[None, :]
    u_prompt = (tj > tc).astype(BF16)
    u_paged = ((tj % B_HEADS == tc % B_HEADS) & (tj // B_HEADS > tc // B_HEADS)).astype(BF16)
    sb_bias = sb_bias0.astype(F32)
    sinks = sinks1.astype(F32)
    convw = conv0.astype(F32)

    i = np.arange(BLK)[:, None]
    c = np.arange(2 * BLK)[None, :]
    d_win = i + BLK - c
    tbl_p = _bias_table(rel_bias, d_win, (d_win >= 0) & (d_win < WINDOW))
    meta_bias = rel_bias[int(_t5_bucket(WINDOW))].astype(F32)
    t_pos = past_len + np.arange(dec_seq)[:, None]
    s_meta = np.arange(N_META)[None, :]
    s_win = past_len - WINDOW + np.arange(WINDOW)[None, :]
    s_new = past_len + np.arange(dec_seq)[None, :]
    tbl_sm = _bias_table(rel_bias, t_pos - s_meta, (t_pos - s_meta) >= WINDOW)
    in_win = lambda d: (d >= 0) & (d < WINDOW)
    tbl_sw = _bias_table(rel_bias, t_pos - s_win, in_win(t_pos - s_win))
    tbl_sn = _bias_table(rel_bias, t_pos - s_new, in_win(t_pos - s_new))

    hp = jnp.concatenate([jnp.zeros((bsz, SEQ_OFF, D_MODEL), F32),
                          jnp.broadcast_to(meta_tokens.astype(F32)[None], (bsz, N_META, D_MODEL)),
                          x_prompt, jnp.zeros((bsz, lp - seq_end, D_MODEL), F32)], axis=1)
    hp = hp.reshape(bsz * lp, D_MODEL)
    hs = x_sample.reshape(db * dec_seq, D_MODEL)
    trp = _row_tile(bsz * lp, 256)
    trs = _row_tile(db * dec_seq, 256)

    qkv_p, z_p, bq_p, bk_p, bv_p, bz_p, ba_p, bkh_p, bvh_p = _inproj0(hp, g0, w0_main, w0_ba, trp)
    qkv_s, z_s, bq_s, bk_s, bv_s, bz_s, ba_s, bkh_s, bvh_s = _inproj0(hs, g0, w0_main, w0_ba, trs)
    seq = lambda a: a.reshape(bsz, lp, a.shape[-1])
    dec = lambda a: a.reshape(db, dec_seq, a.shape[-1])

    oa_p, gdn_p = _gdn_prompt(seq(qkv_p), seq(z_p), seq(ba_p), convw, alog_row, dtb_row, onorm_row, n_real)
    oa_s, gdn_s = _gdn_sample(dec(qkv_s), state_gdn_conv.astype(F32), dec(z_s), dec(ba_s), state_gdn.astype(F32),
                              convw, alog_row, dtb_row, onorm_row)
    ob_p = _sb_prompt(sb_bias, seq(bq_p), seq(bkh_p), seq(bvh_p), seq(bz_p), u_prompt)
    q_rows = bq_s.reshape(db, dec_seq, B_HEADS, B_DIM).transpose(0, 2, 1, 3).reshape(db, B_HEADS * dec_seq, B_DIM)
    pad_rows = lambda a: jnp.pad(a.reshape(db, dec_seq * B_HEADS, B_DIM),
                                 ((0, 0), (0, SB_BLK - dec_seq * B_HEADS), (0, 0)))
    pages = lambda a: a.reshape(a.shape[0], PAGE_ROWS, B_DIM)
    ob_s = _sb_sample(page_table, sb_bias, q_rows, pad_rows(bkh_s), pad_rows(bvh_s), dec(bz_s), u_paged,
                      pages(cache_sb_k), pages(cache_sb_v))

    h1_p, q1_p, k1_p, v1_p, gate_p, k1h_p, v1h_p = _mid(hp, oa_p.reshape(bsz * lp, A_WIDTH),
                                                         ob_p.reshape(bsz * lp, B_WIDTH), woa, wob, g1, w1,
                                                         qn_row, kn_row, seg, trp)
    h1_s, q1_s, k1_s, v1_s, gate_s, k1h_s, v1h_s = _mid(hs, oa_s.reshape(db * dec_seq, A_WIDTH),
                                                         ob_s.reshape(db * dec_seq, B_WIDTH), woa, wob, g1, w1,
                                                         qn_row, kn_row, seg, trs)

    oc_p = _swa_prompt(sinks, meta_bias, seq(q1_p), seq(k1h_p), seq(v1h_p), seq(gate_p), tbl_p)
    flat_kv = lambda a: a.reshape(a.shape[0], a.shape[1], C_KV_WIDTH)
    oc_s = _swa_sample(sinks, dec(q1_s), flat_kv(cache_swa_meta_k), flat_kv(cache_swa_k), dec(k1h_s),
                       flat_kv(cache_swa_meta_v), flat_kv(cache_swa_v), dec(v1h_s), dec(gate_s),
                       tbl_sm, tbl_sw, tbl_sn)
    y_p = _outproj(h1_p, oc_p.reshape(bsz * lp, C_WIDTH), wo1, trp)
    y_s = _outproj(h1_s, oc_s.reshape(db * dec_seq, C_WIDTH), wo1, trs)

    first = SEQ_OFF
    real0 = SEQ_OFF + N_META
    y_prompt = seq(y_p)[:, real0:seq_end]
    y_sample = dec(y_s)
    heads_b = lambda a: a.reshape(a.shape[0], a.shape[1], B_HEADS, B_DIM)
    heads_c = lambda a: a.reshape(a.shape[0], a.shape[1], C_KV_HEADS, C_DIM)
    sbk_p = heads_b(seq(bk_p)[:, first:seq_end])
    sbv_p = heads_b(seq(bv_p)[:, first:seq_end])
    sbk_s = heads_b(dec(bk_s))
    sbv_s = heads_b(dec(bv_s))
    conv_p = seq(qkv_p)[:, seq_end - (A_CONV - 1):seq_end]
    conv_s = jnp.concatenate([state_gdn_conv.astype(F32), dec(qkv_s)], axis=1)[:, -(A_CONV - 1):]
    k1_seq, v1_seq = seq(k1_p), seq(v1_p)
    swk_p = heads_c(k1_seq[:, seq_end - WINDOW:seq_end])
    swv_p = heads_c(v1_seq[:, seq_end - WINDOW:seq_end])
    swmk_p = heads_c(k1_seq[:, first:first + N_META])
    swmv_p = heads_c(v1_seq[:, first:first + N_META])
    swk_s = jnp.concatenate([cache_swa_k.astype(F32), heads_c(dec(k1_s))], axis=1)[:, -WINDOW:]
    swv_s = jnp.concatenate([cache_swa_v.astype(F32), heads_c(dec(v1_s))], axis=1)[:, -WINDOW:]
    return (y_prompt, y_sample, sbk_p, sbv_p, sbk_s, sbv_s, gdn_p, gdn_s, conv_p, conv_s,
            swk_p, swv_p, swmk_p, swmv_p, swk_s, swv_s)
```

```python
import functools
import math

import jax
import jax.numpy as jnp
import numpy as np
from jax import lax
from jax.experimental import pallas as pl
from jax.experimental.pallas import tpu as pltpu

F32 = jnp.float32
BF16 = jnp.bfloat16
HIGHEST = lax.Precision.HIGHEST

D_MODEL = 1024
N_META = 16
EPS = 1e-6
A_HEADS = 4
A_DK = 128
A_DV = 128
A_CONV = 4
A_CHUNK = 64
A_WIDTH = A_HEADS * A_DV
A_QKV = 2 * A_HEADS * A_DK + A_HEADS * A_DV
B_HEADS = 4
B_DIM = 128
B_WIDTH = B_HEADS * B_DIM
C_HEADS = 16
C_KV_HEADS = 4
C_DIM = 64
C_WIDTH = C_HEADS * C_DIM
C_KV_WIDTH = C_KV_HEADS * C_DIM
WINDOW = 128
N_BUCKETS = 32
MAX_DISTANCE = 128
PAGE_SIZE = 128

LANES = 128
BLK = 128
SB_BLK = 256
SEQ_OFF = A_CHUNK - N_META
NEG = -1e30
PAGES_PER_STEP = 16
PAGE_SLOTS = 3
PAGE_ROWS = PAGE_SIZE * B_HEADS
GDN_SEQS_PER_STEP = 4
ROW_TILE = 512
VMEM_LIMIT = 56 * 1024 * 1024

NN = (((1,), (0,)), ((), ()))
NT = (((1,), (1,)), ((), ()))
TN = (((0,), (0,)), ((), ()))


def _dot(a, b, dims=NN):
    return lax.dot_general(a, b, dims, preferred_element_type=F32)


def _silu(x):
    return x * (1.0 / (1.0 + jnp.exp(-x)))


def _softplus(x):
    return jnp.maximum(x, 0.0) + jnp.log(1.0 + jnp.exp(-jnp.abs(x)))


def _split_bf16(x):
    hi = x.astype(BF16)
    lo = (x - hi.astype(F32)).astype(BF16)
    return hi, lo


def _dot3(a, b, dims=NN):
    ah, al = _split_bf16(a)
    bh, bl = _split_bf16(b)
    return _dot(ah, bh, dims) + (_dot(ah, bl, dims) + _dot(al, bh, dims))


def _dot_sel(sel, x):
    s = sel.astype(BF16)
    h1 = x.astype(BF16)
    r1 = x - h1.astype(F32)
    h2 = r1.astype(BF16)
    h3 = (r1 - h2.astype(F32)).astype(BF16)
    return _dot(s, h1) + (_dot(s, h2) + _dot(s, h3))


def _params(*sem):
    return pltpu.CompilerParams(dimension_semantics=sem, vmem_limit_bytes=VMEM_LIMIT)


def _rows(tr, width):
    return pl.BlockSpec((tr, width), lambda i: (i, 0))


def _resident(shape):
    nd = len(shape)
    return pl.BlockSpec(shape, lambda *_: (0,) * nd)


_L0_COLS = (("qkv", A_QKV), ("z", A_WIDTH), ("bq", B_WIDTH), ("bk", B_WIDTH), ("bv", B_WIDTH),
            ("bz", B_WIDTH))


def _inproj0_kernel(x_ref, g_ref, w_ref, wba_ref, qkv_ref, z_ref, bq_ref, bk_ref, bv_ref, bz_ref,
                    ba_ref, bkh_ref, bvh_ref):
    x = x_ref[...]
    xn = x * lax.rsqrt(jnp.mean(x * x, axis=-1, keepdims=True) + EPS) * g_ref[...]
    xb = xn.astype(BF16)
    off = 0
    outs = {}
    for (name, width), ref in zip(_L0_COLS, (qkv_ref, z_ref, bq_ref, bk_ref, bv_ref, bz_ref)):
        u = _dot(xb, w_ref[:, off:off + width])
        outs[name] = u
        if name == "bq":
            u = u * (B_DIM ** -0.5)
        ref[...] = u.astype(ref.dtype)
        off += width
    bkh_ref[...] = outs["bk"].astype(BF16)
    bvh_ref[...] = outs["bv"].astype(BF16)
    ba_ref[...] = _dot3(xn, wba_ref[...])


def _inproj0(x, g, w_main, w_ba, tr):
    n = x.shape[0]
    widths = [w for _, w in _L0_COLS]
    out_shape = [jax.ShapeDtypeStruct((n, A_QKV), F32), jax.ShapeDtypeStruct((n, A_WIDTH), F32),
                 jax.ShapeDtypeStruct((n, B_WIDTH), BF16), jax.ShapeDtypeStruct((n, B_WIDTH), F32),
                 jax.ShapeDtypeStruct((n, B_WIDTH), F32), jax.ShapeDtypeStruct((n, B_WIDTH), F32),
                 jax.ShapeDtypeStruct((n, LANES), F32),
                 jax.ShapeDtypeStruct((n, B_WIDTH), BF16), jax.ShapeDtypeStruct((n, B_WIDTH), BF16)]
    out_specs = [_rows(tr, w) for w in widths] + [_rows(tr, LANES), _rows(tr, B_WIDTH), _rows(tr, B_WIDTH)]
    return pl.pallas_call(
        _inproj0_kernel,
        grid=(n // tr,),
        in_specs=[_rows(tr, D_MODEL), _resident((1, D_MODEL)), _resident(w_main.shape), _resident(w_ba.shape)],
        out_specs=out_specs,
        out_shape=out_shape,
        compiler_params=_params("parallel"),
        name="inproj0",
    )(x, g, w_main, w_ba)


def _gdn_chunks(items, alog_row, dtb_row, onorm_row):
    C = items[0][0].shape[0]
    ri = lax.broadcasted_iota(jnp.int32, (C, C), 0)
    ci = lax.broadcasted_iota(jnp.int32, (C, C), 1)
    incl = ri >= ci
    strict = ri > ci
    eye = ri == ci
    eye_f = eye.astype(F32)
    ones = jnp.ones((C, C), F32)
    ch = []
    for c, ba, z, s_list in items:
        beta_all = 1.0 / (1.0 + jnp.exp(-ba))
        g_all = -jnp.exp(alog_row) * _softplus(ba + dtb_row)
        gc_all = _dot_sel(incl, g_all)
        gl_all = gc_all[C - 1:C, :]
        for h in range(A_HEADS):
            q = c[:, h * A_DK:(h + 1) * A_DK]
            k = c[:, (A_HEADS + h) * A_DK:(A_HEADS + h + 1) * A_DK]
            ch.append(dict(
                q=q * lax.rsqrt(jnp.sum(q * q, axis=-1, keepdims=True) + EPS) * (A_DK ** -0.5),
                k=k * lax.rsqrt(jnp.sum(k * k, axis=-1, keepdims=True) + EPS),
                v=c[:, 2 * A_HEADS * A_DK + h * A_DV:2 * A_HEADS * A_DK + (h + 1) * A_DV],
                beta=beta_all[:, h:h + 1], gc=gc_all[:, A_HEADS + h:A_HEADS + h + 1],
                gl=gl_all[:, A_HEADS + h:A_HEADS + h + 1], s=s_list[h], z=z[:, h * A_DV:(h + 1) * A_DV]))
    for x in ch:
        gcb = jnp.broadcast_to(x["gc"], (C, C))
        gcr = _dot_sel(ones, jnp.where(eye, gcb, 0.0))
        x["decay"] = jnp.exp(jnp.where(incl, gcb - gcr, NEG))
        x["kb"] = x["k"] * x["beta"]
    for x in ch:
        x["p"] = -jnp.where(strict, _dot3(x["kb"], x["k"], NT) * x["decay"], 0.0)
        x["t"] = eye_f + x["p"]
    n = 2
    while n < C:
        for x in ch:
            x["p"] = _dot3(x["p"], x["p"])
        for x in ch:
            x["t"] = x["t"] + _dot3(x["t"], x["p"])
        n *= 2
    for x in ch:
        x["eg"] = jnp.exp(x["gc"])
        x["u"] = _dot3(x["t"], x["v"] * x["beta"])
        x["w"] = _dot3(x["t"], x["kb"] * x["eg"])
        x["a_qk"] = _dot3(x["q"], x["k"], NT) * x["decay"]
    for x in ch:
        x["v_new"] = x["u"] - _dot3(x["w"], x["s"])
    for x in ch:
        kd = x["k"] * jnp.exp(x["gl"] - x["gc"])
        x["s_new"] = x["s"] * jnp.exp(x["gl"]) + _dot3(kd, x["v_new"], TN)
    for x in ch:
        o = _dot3(x["q"] * x["eg"], x["s"]) + _dot3(x["a_qk"], x["v_new"])
        o = o * lax.rsqrt(jnp.mean(o * o, axis=-1, keepdims=True) + EPS) * onorm_row
        x["o"] = o * _silu(x["z"])
    results = []
    for i in range(len(items)):
        mine = ch[i * A_HEADS:(i + 1) * A_HEADS]
        results.append(([x["o"] for x in mine], [x["s_new"] for x in mine]))
    return results


def _conv_silu(xbuf_ref, base, rows, convw_ref):
    acc = None
    for j in range(A_CONV):
        term = xbuf_ref[pl.ds(base - (A_CONV - 1) + j, rows), :] * convw_ref[j:j + 1, :]
        acc = term if acc is None else acc + term
    return _silu(acc)


_XPAD = 8


def _gdn_prompt_kernel(n_real_blocks, qkv_ref, z_ref, ba_ref, convw_ref, alog_ref, dtb_ref, onorm_ref,
                       o_ref, s_out_ref, xbuf_ref, s_ref):
    n = pl.program_id(0)
    bsz = qkv_ref.shape[0]
    C = A_CHUNK

    @pl.when(n == 0)
    def _():
        xbuf_ref[:, 0:_XPAD, :] = jnp.zeros((bsz, _XPAD, A_QKV), F32)
        s_ref[...] = jnp.zeros_like(s_ref)

    xbuf_ref[:, _XPAD:_XPAD + C, :] = qkv_ref[...]

    def run(base, rows):
        items = [(_conv_silu(xbuf_ref.at[b], _XPAD + base, rows, convw_ref), ba_ref[b, base:base + rows, :],
                  z_ref[b, base:base + rows, :], [s_ref[b, h] for h in range(A_HEADS)]) for b in range(bsz)]
        results = _gdn_chunks(items, alog_ref[...], dtb_ref[...], onorm_ref[...])
        for b, (outs, new_s) in enumerate(results):
            for h in range(A_HEADS):
                o_ref[b, base:base + rows, h * A_DV:(h + 1) * A_DV] = outs[h]
                s_ref[b, h] = new_s[h]

    @pl.when(n == 0)
    def _():
        o_ref[:, 0:SEQ_OFF, :] = jnp.zeros((bsz, SEQ_OFF, A_WIDTH), F32)
        run(SEQ_OFF, N_META)

    @pl.when(jnp.logical_and(n > 0, n <= n_real_blocks))
    def _():
        run(0, C)

    @pl.when(n > n_real_blocks)
    def _():
        o_ref[...] = jnp.zeros((bsz, C, A_WIDTH), F32)

    @pl.when(n == n_real_blocks)
    def _():
        s_out_ref[...] = s_ref[...]

    xbuf_ref[:, 0:_XPAD, :] = xbuf_ref[:, C:C + _XPAD, :]


def _gdn_prompt(qkv, z, ba, convw, alog_row, dtb_row, onorm_row, n_real):
    bsz, lp, _ = qkv.shape
    C = A_CHUNK
    nblk = lp // C
    blk = lambda w: pl.BlockSpec((bsz, C, w), lambda n: (0, n, 0))
    return pl.pallas_call(
        functools.partial(_gdn_prompt_kernel, n_real // C),
        grid=(nblk,),
        in_specs=[blk(A_QKV), blk(A_WIDTH), blk(LANES), _resident(convw.shape), _resident((1, LANES)),
                  _resident((1, LANES)), _resident((1, A_DV))],
        out_specs=[blk(A_WIDTH), _resident((bsz, A_HEADS, A_DK, A_DV))],
        out_shape=[jax.ShapeDtypeStruct((bsz, lp, A_WIDTH), F32),
                   jax.ShapeDtypeStruct((bsz, A_HEADS, A_DK, A_DV), F32)],
        scratch_shapes=[pltpu.VMEM((bsz, _XPAD + C, A_QKV), F32), pltpu.VMEM((bsz, A_HEADS, A_DK, A_DV), F32)],
        compiler_params=_params("arbitrary"),
        name="gdn_prompt",
    )(qkv, z, ba, convw, alog_row, dtb_row, onorm_row)


def _gdn_sample_kernel(qkv_ref, buf_ref, z_ref, ba_ref, s0_ref, convw_ref, alog_ref, dtb_ref, onorm_ref,
                       o_ref, s_out_ref, xbuf_ref):
    nb, S, _ = qkv_ref.shape
    xbuf_ref[:, 0:_XPAD, :] = jnp.zeros((nb, _XPAD, A_QKV), F32)
    xbuf_ref[:, _XPAD - (A_CONV - 1):_XPAD, :] = buf_ref[...]
    xbuf_ref[:, _XPAD:_XPAD + S, :] = qkv_ref[...]
    items = [(_conv_silu(xbuf_ref.at[b], _XPAD, S, convw_ref), ba_ref[b], z_ref[b],
              [s0_ref[b, h] for h in range(A_HEADS)]) for b in range(nb)]
    results = _gdn_chunks(items, alog_ref[...], dtb_ref[...], onorm_ref[...])
    for b, (outs, new_s) in enumerate(results):
        for h in range(A_HEADS):
            o_ref[b, :, h * A_DV:(h + 1) * A_DV] = outs[h]
            s_out_ref[b, h] = new_s[h]


def _gdn_sample(qkv, buf, z, ba, s0, convw, alog_row, dtb_row, onorm_row):
    db, S, _ = qkv.shape
    nb = GDN_SEQS_PER_STEP if db % GDN_SEQS_PER_STEP == 0 else 1
    blk = lambda r, w: pl.BlockSpec((nb, r, w), lambda b: (b, 0, 0))
    sblk = pl.BlockSpec((nb, A_HEADS, A_DK, A_DV), lambda b: (b, 0, 0, 0))
    return pl.pallas_call(
        _gdn_sample_kernel,
        grid=(db // nb,),
        in_specs=[blk(S, A_QKV), blk(A_CONV - 1, A_QKV), blk(S, A_WIDTH), blk(S, LANES), sblk,
                  _resident(convw.shape), _resident((1, LANES)), _resident((1, LANES)), _resident((1, A_DV))],
        out_specs=[blk(S, A_WIDTH), sblk],
        out_shape=[jax.ShapeDtypeStruct((db, S, A_WIDTH), F32),
                   jax.ShapeDtypeStruct((db, A_HEADS, A_DK, A_DV), F32)],
        scratch_shapes=[pltpu.VMEM((nb, _XPAD + S, A_QKV), F32)],
        compiler_params=_params("parallel"),
        name="gdn_sample",
    )(qkv, buf, z, ba, s0, convw, alog_row, dtb_row, onorm_row)


def _sb_tile(s, bias, mask, u_mat, v_blk, carry, acc):
    z = s + bias
    sp = _softplus(z)
    log_beta = z - sp
    if mask is not None:
        sp = jnp.where(mask, sp, 0.0)
    n = sp.shape[0]
    hi, lo = _split_bf16(sp)
    r2 = _dot(jnp.concatenate([hi, lo], axis=0), u_mat)
    w = jnp.exp(log_beta - (r2[:n] + r2[n:]) - carry)
    if mask is not None:
        w = jnp.where(mask, w, 0.0)
    acc = acc + _dot(w.astype(BF16), v_blk)
    carry = carry + jnp.sum(sp, axis=-1, keepdims=True)
    return carry, acc


def _sb_prompt_kernel(bias_ref, q_ref, k_ref, v_ref, bz_ref, u_ref, o_ref):
    qi = pl.program_id(1)
    T = SB_BLK
    u_mat = u_ref[...]
    rows = lax.broadcasted_iota(jnp.int32, (T, T), 0)
    cols = lax.broadcasted_iota(jnp.int32, (T, T), 1)
    causal = cols < rows

    def tiles(kbs, state, mask):
        heads = range(B_HEADS)
        sls = [pl.ds(h * B_DIM, B_DIM) for h in heads]
        starts = [pl.multiple_of(kb * T, T) for kb in kbs]
        chains = [(j, h) for j in range(len(kbs)) for h in heads]
        z = [_dot(q_ref[0, :, sls[h]], k_ref[0, pl.ds(starts[j], T), sls[h]], NT) + bias_ref[h] for j, h in chains]
        sp = [_softplus(x) for x in z]
        log_beta = [x - y for x, y in zip(z, sp)]
        if mask is not None:
            sp = [jnp.where(mask, x, 0.0) for x in sp]
        parts = []
        for x in sp:
            parts.extend(_split_bf16(x))
        r2 = _dot(jnp.concatenate(parts, axis=0), u_mat)
        sums = [jnp.sum(x, axis=-1, keepdims=True) for x in sp]
        carries = [c for c, _ in state]
        accs = [a for _, a in state]
        for i, (j, h) in enumerate(chains):
            r = r2[2 * T * i:2 * T * i + T] + r2[2 * T * i + T:2 * T * (i + 1)]
            w = jnp.exp(log_beta[i] - r - carries[h])
            if mask is not None:
                w = jnp.where(mask, w, 0.0)
            accs[h] = accs[h] + _dot(w.astype(BF16), v_ref[0, pl.ds(starts[j], T), sls[h]])
            carries[h] = carries[h] + sums[i]
        return tuple(zip(carries, accs))

    state = tuple((jnp.zeros((T, 1), F32), jnp.zeros((T, B_DIM), F32)) for _ in range(B_HEADS))
    state = tiles([qi], state, causal)
    odd = qi % 2
    state = lax.fori_loop(0, odd, lambda it, st: tiles([qi - 1], st, None), state)
    first = qi - 1 - odd
    state = lax.fori_loop(0, qi // 2, lambda it, st: tiles([first - 2 * it, first - 2 * it - 1], st, None), state)
    for h in range(B_HEADS):
        sl = pl.ds(h * B_DIM, B_DIM)
        o_ref[0, :, sl] = state[h][1] * _silu(bz_ref[0, :, sl])


def _sb_prompt(sb_bias, q, k, v, bz, u_mat):
    bsz, lp, _ = q.shape
    nq = lp // SB_BLK
    qblk = pl.BlockSpec((1, SB_BLK, B_WIDTH), lambda b, i: (b, i, 0))
    kvblk = pl.BlockSpec((1, lp, B_WIDTH), lambda b, i: (b, 0, 0))
    return pl.pallas_call(
        _sb_prompt_kernel,
        grid=(bsz, nq),
        in_specs=[pl.BlockSpec(memory_space=pltpu.SMEM), qblk, kvblk, kvblk, qblk, _resident(u_mat.shape)],
        out_specs=qblk,
        out_shape=jax.ShapeDtypeStruct((bsz, lp, B_WIDTH), F32),
        compiler_params=_params("parallel", "arbitrary"),
        name="sb_prompt",
    )(sb_bias, q, k, v, bz, u_mat)


def _sb_sample_kernel(pt_ref, bias_ref, q_ref, kn_ref, vn_ref, bz_ref, u_ref, ck_hbm, cv_hbm,
                      o_ref, carry_ref, acc_ref, kbuf_ref, vbuf_ref, kpg_ref, vpg_ref, sem_ref):
    npg = PAGES_PER_STEP
    c = pl.program_id(1)
    nch = pl.num_programs(1)
    g = pl.program_id(0) * nch + c
    n_steps = pl.num_programs(0) * nch
    ahead = PAGE_SLOTS - 1

    def page_copies(step):
        slot = step % PAGE_SLOTS
        sb = step // nch
        first = (nch - 1 - step % nch) * npg
        copies = []
        for p in range(npg):
            page = pt_ref[sb, first + p]
            copies.append(pltpu.make_async_copy(ck_hbm.at[page], kpg_ref.at[slot, p], sem_ref.at[slot, 0]))
            copies.append(pltpu.make_async_copy(cv_hbm.at[page], vpg_ref.at[slot, p], sem_ref.at[slot, 1]))
        return copies

    @pl.when(g == 0)
    def _():
        for j in range(ahead):
            for cp in page_copies(j):
                cp.start()

    @pl.when(g + ahead < n_steps)
    def _():
        for cp in page_copies(g + ahead):
            cp.start()

    for cp in page_copies(g):
        cp.wait()
    slot = g % PAGE_SLOTS
    T = SB_BLK
    R = q_ref.shape[1]
    S = R // B_HEADS
    u_mat = u_ref[...]
    q = q_ref[0]
    rows = lax.broadcasted_iota(jnp.int32, (R, T), 0)
    cols = lax.broadcasted_iota(jnp.int32, (R, T), 1)
    row_head = rows // S
    same_head = (cols % B_HEADS) == row_head
    rid = lax.broadcasted_iota(jnp.int32, (R, 1), 0) // S
    bias = jnp.zeros((R, 1), F32)
    for h in range(B_HEADS):
        bias = jnp.where(rid == h, bias_ref[h], bias)

    @pl.when(c == 0)
    def _():
        new_mask = jnp.logical_and(same_head, cols // B_HEADS < rows % S)
        carry, acc = _sb_tile(_dot(q, kn_ref[0], NT), bias, new_mask, u_mat, vn_ref[0],
                              jnp.zeros((R, 1), F32), jnp.zeros((R, B_DIM), F32))
        carry_ref[...] = jnp.broadcast_to(carry, carry_ref.shape)
        acc_ref[...] = acc

    for p in range(npg):
        kbuf_ref[p * PAGE_ROWS:(p + 1) * PAGE_ROWS, :] = kpg_ref[slot, p].astype(BF16)
        vbuf_ref[p * PAGE_ROWS:(p + 1) * PAGE_ROWS, :] = vpg_ref[slot, p].astype(BF16)
    ncol = npg * PAGE_ROWS
    nt = ncol // T
    carry_in = carry_ref[:, 0:1]
    z = _dot(q, kbuf_ref[...], NT) + bias
    sp = _softplus(z)
    log_beta = z - sp
    match = (lax.broadcasted_iota(jnp.int32, (R, ncol), 1) % B_HEADS) == (
        lax.broadcasted_iota(jnp.int32, (R, ncol), 0) // S)
    sp = jnp.where(match, sp, 0.0)
    hi, lo = _split_bf16(sp)
    stacked = jnp.concatenate([part[:, t * T:(t + 1) * T] for t in range(nt) for part in (hi, lo)], axis=0)
    r2 = _dot(stacked, u_mat)
    later = carry_in
    ws = [None] * nt
    for t in reversed(range(nt)):
        cs = slice(t * T, (t + 1) * T)
        r = r2[2 * R * t:2 * R * t + R] + r2[2 * R * t + R:2 * R * (t + 1)]
        ws[t] = jnp.exp(log_beta[:, cs] - r - later)
        later = later + jnp.sum(sp[:, cs], axis=-1, keepdims=True)
    w = jnp.where(match, jnp.concatenate(ws, axis=1), 0.0)
    acc = acc_ref[...] + _dot(w.astype(BF16), vbuf_ref[...])
    carry_ref[...] = jnp.broadcast_to(later, carry_ref.shape)
    acc_ref[...] = acc

    @pl.when(c == pl.num_programs(1) - 1)
    def _():
        for h in range(B_HEADS):
            sl = slice(h * B_DIM, (h + 1) * B_DIM)
            o_ref[0, :, sl] = acc[h * S:(h + 1) * S, :] * _silu(bz_ref[0, :, sl])


def _sb_sample(page_table, sb_bias, q_rows, kn_rows, vn_rows, bz, u_mat, cache_k, cache_v):
    db, R, _ = q_rows.shape
    S = R // B_HEADS
    n_pages = page_table.shape[1]
    npg = PAGES_PER_STEP
    assert n_pages % npg == 0 and PAGE_ROWS % SB_BLK == 0
    nch = n_pages // npg
    assert db * nch >= PAGE_SLOTS - 1
    per_seq = lambda r, w: pl.BlockSpec((1, r, w), lambda b, c, pt: (b, 0, 0))
    hbm = pl.BlockSpec(memory_space=pl.ANY)
    page_set = (PAGE_SLOTS, npg, PAGE_ROWS, B_DIM)
    grid_spec = pltpu.PrefetchScalarGridSpec(
        num_scalar_prefetch=1,
        grid=(db, nch),
        in_specs=[pl.BlockSpec(memory_space=pltpu.SMEM), per_seq(R, B_DIM), per_seq(SB_BLK, B_DIM),
                  per_seq(SB_BLK, B_DIM), per_seq(S, B_WIDTH), pl.BlockSpec(u_mat.shape, lambda b, c, pt: (0, 0)),
                  hbm, hbm],
        out_specs=per_seq(S, B_WIDTH),
        scratch_shapes=[pltpu.VMEM((R, LANES), F32), pltpu.VMEM((R, B_DIM), F32),
                        pltpu.VMEM((npg * PAGE_ROWS, B_DIM), BF16), pltpu.VMEM((npg * PAGE_ROWS, B_DIM), BF16),
                        pltpu.VMEM(page_set, F32), pltpu.VMEM(page_set, F32),
                        pltpu.SemaphoreType.DMA((PAGE_SLOTS, 2))],
    )
    return pl.pallas_call(
        _sb_sample_kernel,
        grid_spec=grid_spec,
        out_shape=jax.ShapeDtypeStruct((db, S, B_WIDTH), F32),
        compiler_params=_params("arbitrary", "arbitrary"),
        name="sb_sample",
    )(page_table, sb_bias, q_rows, kn_rows, vn_rows, bz, u_mat, cache_k, cache_v)


def _seg_mean_sq(x, seg_ones):
    hi, lo = _split_bf16(x * x)
    w = seg_ones.shape[0]
    parts = []
    for c0 in range(0, x.shape[1], w):
        parts.append(_dot(hi[:, c0:c0 + w], seg_ones) + _dot(lo[:, c0:c0 + w], seg_ones))
    ss = parts[0] if len(parts) == 1 else jnp.concatenate(parts, axis=1)
    return ss * (1.0 / C_DIM)


def _mid_kernel(h_ref, oa_ref, ob_ref, woa_ref, wob_ref, g_ref, w_ref, qn_ref, kn_ref, seg_ref,
                h1_ref, q_ref, k_ref, v_ref, gate_ref, kh_ref, vh_ref):
    h1 = h_ref[...] + _dot(oa_ref[...].astype(BF16), woa_ref[...]) + _dot(ob_ref[...].astype(BF16), wob_ref[...])
    h1_ref[...] = h1
    xn = h1 * lax.rsqrt(jnp.mean(h1 * h1, axis=-1, keepdims=True) + EPS) * g_ref[...]
    xb = xn.astype(BF16)
    seg = seg_ref[...]
    q = _dot(xb, w_ref[:, 0:C_WIDTH])
    q = q * lax.rsqrt(_seg_mean_sq(q, seg) + EPS) * qn_ref[...]
    q_ref[...] = q.astype(BF16)
    k = _dot(xb, w_ref[:, C_WIDTH:C_WIDTH + C_KV_WIDTH])
    k = k * lax.rsqrt(_seg_mean_sq(k, seg) + EPS) * kn_ref[...]
    k_ref[...] = k
    kh_ref[...] = k.astype(BF16)
    v = _dot(xb, w_ref[:, C_WIDTH + C_KV_WIDTH:C_WIDTH + 2 * C_KV_WIDTH])
    v_ref[...] = v
    vh_ref[...] = v.astype(BF16)
    gate_ref[...] = _silu(_dot(xb, w_ref[:, C_WIDTH + 2 * C_KV_WIDTH:2 * C_WIDTH + 2 * C_KV_WIDTH]))


def _mid(h, oa, ob, woa, wob, g, w1, qn_row, kn_row, seg, tr):
    n = h.shape[0]
    sds = jax.ShapeDtypeStruct
    return pl.pallas_call(
        _mid_kernel,
        grid=(n // tr,),
        in_specs=[_rows(tr, D_MODEL), _rows(tr, A_WIDTH), _rows(tr, B_WIDTH), _resident(woa.shape),
                  _resident(wob.shape), _resident((1, D_MODEL)), _resident(w1.shape), _resident((1, C_WIDTH)),
                  _resident((1, C_KV_WIDTH)), _resident(seg.shape)],
        out_specs=[_rows(tr, D_MODEL), _rows(tr, C_WIDTH), _rows(tr, C_KV_WIDTH), _rows(tr, C_KV_WIDTH),
                   _rows(tr, C_WIDTH), _rows(tr, C_KV_WIDTH), _rows(tr, C_KV_WIDTH)],
        out_shape=[sds((n, D_MODEL), F32), sds((n, C_WIDTH), BF16), sds((n, C_KV_WIDTH), F32),
                   sds((n, C_KV_WIDTH), F32), sds((n, C_WIDTH), F32), sds((n, C_KV_WIDTH), BF16),
                   sds((n, C_KV_WIDTH), BF16)],
        compiler_params=_params("parallel"),
        name="mid",
    )(h, oa, ob, woa, wob, g, w1, qn_row, kn_row, seg)


def _swa_heads(qs, pieces_per_head, sinks):
    heads = range(len(qs))
    scores = [[_dot(qs[h], k, NT) * (C_DIM ** -0.5) + b for k, _, b in pieces_per_head[h]] for h in heads]
    mx = []
    for h in heads:
        m = sinks[h]
        for s in scores[h]:
            m = jnp.maximum(m, jnp.max(s, axis=-1, keepdims=True))
        mx.append(m)
    probs = [[jnp.exp(s - mx[h]) for s in scores[h]] for h in heads]
    outs = []
    for h in heads:
        denom = jnp.exp(sinks[h] - mx[h])
        o = None
        for p, (_, v, _) in zip(probs[h], pieces_per_head[h]):
            denom = denom + jnp.sum(p, axis=-1, keepdims=True)
            t = _dot(p.astype(BF16), v)
            o = t if o is None else o + t
        outs.append(o / denom)
    return outs


def _swa_prompt_kernel(sink_ref, mbias_ref, q_ref, kp_ref, ko_ref, k0_ref, vp_ref, vo_ref, v0_ref, gate_ref,
                       tbl_ref, o_ref):
    qi = pl.program_id(1)
    col = lax.broadcasted_iota(jnp.int32, (1, BLK), 1)
    own_ok = col >= jnp.where(qi == 0, SEQ_OFF, 0)
    prev_ok = col >= jnp.where(qi == 0, BLK, jnp.where(qi == 1, SEQ_OFF, 0))
    row = lax.broadcasted_iota(jnp.int32, (BLK, N_META), 0)
    mcol = lax.broadcasted_iota(jnp.int32, (BLK, N_META), 1)
    meta_ok = qi * BLK + row - (SEQ_OFF + mcol) >= WINDOW
    sls = [slice(h * C_DIM, (h + 1) * C_DIM) for h in range(C_HEADS)]
    pieces = []
    for h in range(C_HEADS):
        g = h // (C_HEADS // C_KV_HEADS)
        kv = slice(g * C_DIM, (g + 1) * C_DIM)
        pieces.append([
            (kp_ref[0, :, kv], vp_ref[0, :, kv], jnp.where(prev_ok, tbl_ref[h, :, 0:BLK], NEG)),
            (ko_ref[0, :, kv], vo_ref[0, :, kv], jnp.where(own_ok, tbl_ref[h, :, BLK:2 * BLK], NEG)),
            (k0_ref[0, SEQ_OFF:SEQ_OFF + N_META, kv], v0_ref[0, SEQ_OFF:SEQ_OFF + N_META, kv],
             jnp.where(meta_ok, mbias_ref[h], NEG)),
        ])
    outs = _swa_heads([q_ref[0, :, sl] for sl in sls], pieces, [sink_ref[h] for h in range(C_HEADS)])
    for h, sl in enumerate(sls):
        o_ref[0, :, sl] = (outs[h] * gate_ref[0, :, sl]).astype(o_ref.dtype)


def _swa_prompt(sinks, meta_bias, q, k, v, gate, tbl):
    bsz, lp, _ = q.shape
    nq = lp // BLK
    qblk = pl.BlockSpec((1, BLK, C_WIDTH), lambda b, i: (b, i, 0))
    own = pl.BlockSpec((1, BLK, C_KV_WIDTH), lambda b, i: (b, i, 0))
    prev = pl.BlockSpec((1, BLK, C_KV_WIDTH), lambda b, i: (b, jnp.maximum(i - 1, 0), 0))
    first = pl.BlockSpec((1, BLK, C_KV_WIDTH), lambda b, i: (b, 0, 0))
    smem = pl.BlockSpec(memory_space=pltpu.SMEM)
    return pl.pallas_call(
        _swa_prompt_kernel,
        grid=(bsz, nq),
        in_specs=[smem, smem, qblk, prev, own, first, prev, own, first, qblk, _resident(tbl.shape)],
        out_specs=qblk,
        out_shape=jax.ShapeDtypeStruct((bsz, lp, C_WIDTH), BF16),
        compiler_params=_params("parallel", "parallel"),
        name="swa_prompt",
    )(sinks, meta_bias, q, k, k, k, v, v, v, gate, tbl)


def _swa_sample_kernel(sink_ref, q_ref, km_ref, kw_ref, kn_ref, vm_ref, vw_ref, vn_ref, gate_ref,
                       tm_ref, tw_ref, tn_ref, o_ref):
    sls = [slice(h * C_DIM, (h + 1) * C_DIM) for h in range(C_HEADS)]
    kvs = [slice(g * C_DIM, (g + 1) * C_DIM) for g in range(C_KV_HEADS)]
    cached = [[(km_ref[0, :, kv].astype(BF16), vm_ref[0, :, kv].astype(BF16)),
               (kw_ref[0, :, kv].astype(BF16), vw_ref[0, :, kv].astype(BF16)),
               (kn_ref[0, :, kv], vn_ref[0, :, kv])] for kv in kvs]
    pieces = []
    for h in range(C_HEADS):
        g = h // (C_HEADS // C_KV_HEADS)
        pieces.append([(k, v, t[h]) for (k, v), t in zip(cached[g], (tm_ref, tw_ref, tn_ref))])
    outs = _swa_heads([q_ref[0, :, sl] for sl in sls], pieces, [sink_ref[h] for h in range(C_HEADS)])
    for h, sl in enumerate(sls):
        o_ref[0, :, sl] = (outs[h] * gate_ref[0, :, sl]).astype(o_ref.dtype)


def _swa_sample(sinks, q, km, kw, kn, vm, vw, vn, gate, tm, tw, tn):
    db, S, _ = q.shape
    blk = lambda r, w: pl.BlockSpec((1, r, w), lambda b: (b, 0, 0))
    return pl.pallas_call(
        _swa_sample_kernel,
        grid=(db,),
        in_specs=[pl.BlockSpec(memory_space=pltpu.SMEM), blk(S, C_WIDTH),
                  blk(N_META, C_KV_WIDTH), blk(WINDOW, C_KV_WIDTH), blk(S, C_KV_WIDTH),
                  blk(N_META, C_KV_WIDTH), blk(WINDOW, C_KV_WIDTH), blk(S, C_KV_WIDTH),
                  blk(S, C_WIDTH), _resident(tm.shape), _resident(tw.shape), _resident(tn.shape)],
        out_specs=blk(S, C_WIDTH),
        out_shape=jax.ShapeDtypeStruct((db, S, C_WIDTH), BF16),
        compiler_params=_params("parallel"),
        name="swa_sample",
    )(sinks, q, km, kw, kn, vm, vw, vn, gate, tm, tw, tn)


def _outproj_kernel(h_ref, o_ref, w_ref, y_ref):
    y_ref[...] = h_ref[...] + _dot(o_ref[...], w_ref[...])


def _outproj(h, o, w, tr):
    n = h.shape[0]
    return pl.pallas_call(
        _outproj_kernel,
        grid=(n // tr,),
        in_specs=[_rows(tr, D_MODEL), _rows(tr, C_WIDTH), _resident(w.shape)],
        out_specs=_rows(tr, D_MODEL),
        out_shape=jax.ShapeDtypeStruct((n, D_MODEL), F32),
        compiler_params=_params("parallel"),
        name="outproj1",
    )(h, o, w)


def _t5_bucket(d):
    d = np.maximum(np.asarray(d, np.int64), 0)
    exact = N_BUCKETS // 2
    large = exact + (np.log(np.maximum(d, 1) / exact) / math.log(MAX_DISTANCE / exact)
                     * (N_BUCKETS - exact)).astype(np.int64)
    return np.where(d < exact, d, np.minimum(large, N_BUCKETS - 1))


def _bias_table(rel_bias, d, allow):
    onehot = (_t5_bucket(d)[None] == np.arange(N_BUCKETS)[:, None, None]) & np.asarray(allow)[None]
    b = jnp.einsum("nh,nqk->hqk", rel_bias.astype(F32), jnp.asarray(onehot, F32), precision=HIGHEST)
    return jnp.where(jnp.asarray(np.asarray(allow))[None], b, NEG)


def _row_tile(n, cap):
    t = cap
    while n % t:
        t //= 2
    return t


def kernel(x_prompt, x_sample, cache_sb_k, cache_sb_v, page_table, state_gdn, state_gdn_conv, cache_swa_k,
           cache_swa_v, cache_swa_meta_k, cache_swa_meta_v, meta_tokens, rel_bias, norm0, w_in0, conv0, a_log0,
           dt_bias0, gdn_norm0, sb_bias0, w_out0, norm1, w_in1, q_norm1, k_norm1, sinks1, w_out1):
    bsz, n_real, _ = x_prompt.shape
    db, dec_seq, _ = x_sample.shape
    past_len = page_table.shape[1] * PAGE_SIZE
    assert n_real % SB_BLK == 0 and SB_BLK % BLK == 0 and BLK % A_CHUNK == 0
    seq_end = SEQ_OFF + N_META + n_real
    lp = n_real + SB_BLK

    a_end = A_QKV + A_WIDTH
    w0_main = jnp.concatenate([w_in0[:, :a_end], w_in0[:, a_end + 2 * A_HEADS:]], axis=1).astype(BF16)
    w0_ba = jnp.pad(w_in0[:, a_end:a_end + 2 * A_HEADS], ((0, 0), (0, LANES - 2 * A_HEADS)))
    lane_row = lambda vec, off: jnp.zeros((1, LANES), F32).at[0, off:off + vec.shape[0]].set(vec.astype(F32))
    alog_row = lane_row(a_log0, A_HEADS)
    dtb_row = lane_row(dt_bias0, A_HEADS)
    onorm_row = gdn_norm0.astype(F32)[None]
    g0 = norm0.astype(F32)[None]
    g1 = norm1.astype(F32)[None]
    woa = w_out0[:A_WIDTH].astype(BF16)
    wob = w_out0[A_WIDTH:].astype(BF16)
    w1 = w_in1.astype(BF16)
    wo1 = w_out1.astype(BF16)
    qn_row = jnp.tile(q_norm1.astype(F32), C_HEADS)[None]
    kn_row = jnp.tile(k_norm1.astype(F32), C_KV_HEADS)[None]
    seg_w = 2 * LANES
    seg = (jnp.arange(seg_w)[:, None] // C_DIM == jnp.arange(seg_w)[None, :] // C_DIM).astype(BF16)
    tj = jnp.arange(SB_BLK)[:, None]
    tc = jnp.arange(SB_BLK)[None, :]
    u_prompt = (tj > tc).astype(BF16)
    u_paged = ((tj % B_HEADS == tc % B_HEADS) & (tj // B_HEADS > tc // B_HEADS)).astype(BF16)
    sb_bias = sb_bias0.astype(F32)
    sinks = sinks1.astype(F32)
    convw = conv0.astype(F32)

    i = np.arange(BLK)[:, None]
    c = np.arange(2 * BLK)[None, :]
    d_win = i + BLK - c
    tbl_p = _bias_table(rel_bias, d_win, (d_win >= 0) & (d_win < WINDOW))
    meta_bias = rel_bias[int(_t5_bucket(WINDOW))].astype(F32)
    t_pos = past_len + np.arange(dec_seq)[:, None]
    s_meta = np.arange(N_META)[None, :]
    s_win = past_len - WINDOW + np.arange(WINDOW)[None, :]
    s_new = past_len + np.arange(dec_seq)[None, :]
    tbl_sm = _bias_table(rel_bias, t_pos - s_meta, (t_pos - s_meta) >= WINDOW)
    in_win = lambda d: (d >= 0) & (d < WINDOW)
    tbl_sw = _bias_table(rel_bias, t_pos - s_win, in_win(t_pos - s_win))
    tbl_sn = _bias_table(rel_bias, t_pos - s_new, in_win(t_pos - s_new))

    hp = jnp.concatenate([jnp.zeros((bsz, SEQ_OFF, D_MODEL), F32),
                          jnp.broadcast_to(meta_tokens.astype(F32)[None], (bsz, N_META, D_MODEL)),
                          x_prompt, jnp.zeros((bsz, lp - seq_end, D_MODEL), F32)], axis=1)
    hp = hp.reshape(bsz * lp, D_MODEL)
    hs = x_sample.reshape(db * dec_seq, D_MODEL)
    trp = _row_tile(bsz * lp, ROW_TILE)
    trs = _row_tile(db * dec_seq, ROW_TILE)

    qkv_p, z_p, bq_p, bk_p, bv_p, bz_p, ba_p, bkh_p, bvh_p = _inproj0(hp, g0, w0_main, w0_ba, trp)
    qkv_s, z_s, bq_s, bk_s, bv_s, bz_s, ba_s, bkh_s, bvh_s = _inproj0(hs, g0, w0_main, w0_ba, trs)
    seq = lambda a: a.reshape(bsz, lp, a.shape[-1])
    dec = lambda a: a.reshape(db, dec_seq, a.shape[-1])

    oa_p, gdn_p = _gdn_prompt(seq(qkv_p), seq(z_p), seq(ba_p), convw, alog_row, dtb_row, onorm_row, n_real)
    oa_s, gdn_s = _gdn_sample(dec(qkv_s), state_gdn_conv.astype(F32), dec(z_s), dec(ba_s), state_gdn.astype(F32),
                              convw, alog_row, dtb_row, onorm_row)
    ob_p = _sb_prompt(sb_bias, seq(bq_p), seq(bkh_p), seq(bvh_p), seq(bz_p), u_prompt)
    q_rows = bq_s.reshape(db, dec_seq, B_HEADS, B_DIM).transpose(0, 2, 1, 3).reshape(db, B_HEADS * dec_seq, B_DIM)
    pad_rows = lambda a: jnp.pad(a.reshape(db, dec_seq * B_HEADS, B_DIM),
                                 ((0, 0), (0, SB_BLK - dec_seq * B_HEADS), (0, 0)))
    pages = lambda a: a.reshape(a.shape[0], PAGE_ROWS, B_DIM)
    ob_s = _sb_sample(page_table, sb_bias, q_rows, pad_rows(bkh_s), pad_rows(bvh_s), dec(bz_s), u_paged,
                      pages(cache_sb_k), pages(cache_sb_v))

    h1_p, q1_p, k1_p, v1_p, gate_p, k1h_p, v1h_p = _mid(hp, oa_p.reshape(bsz * lp, A_WIDTH),
                                                         ob_p.reshape(bsz * lp, B_WIDTH), woa, wob, g1, w1,
                                                         qn_row, kn_row, seg, trp)
    h1_s, q1_s, k1_s, v1_s, gate_s, k1h_s, v1h_s = _mid(hs, oa_s.reshape(db * dec_seq, A_WIDTH),
                                                         ob_s.reshape(db * dec_seq, B_WIDTH), woa, wob, g1, w1,
                                                         qn_row, kn_row, seg, trs)

    oc_p = _swa_prompt(sinks, meta_bias, seq(q1_p), seq(k1h_p), seq(v1h_p), seq(gate_p), tbl_p)
    flat_kv = lambda a: a.reshape(a.shape[0], a.shape[1], C_KV_WIDTH)
    oc_s = _swa_sample(sinks, dec(q1_s), flat_kv(cache_swa_meta_k), flat_kv(cache_swa_k), dec(k1h_s),
                       flat_kv(cache_swa_meta_v), flat_kv(cache_swa_v), dec(v1h_s), dec(gate_s),
                       tbl_sm, tbl_sw, tbl_sn)
    y_p = _outproj(h1_p, oc_p.reshape(bsz * lp, C_WIDTH), wo1, trp)
    y_s = _outproj(h1_s, oc_s.reshape(db * dec_seq, C_WIDTH), wo1, trs)

    first = SEQ_OFF
    real0 = SEQ_OFF + N_META
    y_prompt = seq(y_p)[:, real0:seq_end]
    y_sample = dec(y_s)
    heads_b = lambda a: a.reshape(a.shape[0], a.shape[1], B_HEADS, B_DIM)
    heads_c = lambda a: a.reshape(a.shape[0], a.shape[1], C_KV_HEADS, C_DIM)
    sbk_p = heads_b(seq(bk_p)[:, first:seq_end])
    sbv_p = heads_b(seq(bv_p)[:, first:seq_end])
    sbk_s = heads_b(dec(bk_s))
    sbv_s = heads_b(dec(bv_s))
    conv_p = seq(qkv_p)[:, seq_end - (A_CONV - 1):seq_end]
    conv_s = jnp.concatenate([state_gdn_conv.astype(F32), dec(qkv_s)], axis=1)[:, -(A_CONV - 1):]
    k1_seq, v1_seq = seq(k1_p), seq(v1_p)
    swk_p = heads_c(k1_seq[:, seq_end - WINDOW:seq_end])
    swv_p = heads_c(v1_seq[:, seq_end - WINDOW:seq_end])
    swmk_p = heads_c(k1_seq[:, first:first + N_META])
    swmv_p = heads_c(v1_seq[:, first:first + N_META])
    swk_s = jnp.concatenate([cache_swa_k.astype(F32), heads_c(dec(k1_s))], axis=1)[:, -WINDOW:]
    swv_s = jnp.concatenate([cache_swa_v.astype(F32), heads_c(dec(v1_s))], axis=1)[:, -WINDOW:]
    return (y_prompt, y_sample, sbk_p, sbv_p, sbk_s, sbv_s, gdn_p, gdn_s, conv_p, conv_s,
            swk_p, swv_p, swmk_p, swmv_p, swk_s, swv_s)
```

```python
import functools
import math

import jax
import jax.numpy as jnp
import numpy as np
from jax import lax
from jax.experimental import pallas as pl
from jax.experimental.pallas import tpu as pltpu

F32 = jnp.float32
BF16 = jnp.bfloat16
HIGHEST = lax.Precision.HIGHEST

D_MODEL = 1024
N_META = 16
EPS = 1e-6
A_HEADS = 4
A_DK = 128
A_DV = 128
A_CONV = 4
A_CHUNK = 64
A_WIDTH = A_HEADS * A_DV
A_QKV = 2 * A_HEADS * A_DK + A_HEADS * A_DV
B_HEADS = 4
B_DIM = 128
B_WIDTH = B_HEADS * B_DIM
C_HEADS = 16
C_KV_HEADS = 4
C_DIM = 64
C_WIDTH = C_HEADS * C_DIM
C_KV_WIDTH = C_KV_HEADS * C_DIM
WINDOW = 128
N_BUCKETS = 32
MAX_DISTANCE = 128
PAGE_SIZE = 128

LANES = 128
BLK = 128
SB_BLK = 256
SEQ_OFF = A_CHUNK - N_META
NEG = -1e30
PAGES_PER_STEP = 16
PAGE_SLOTS = 3
PAGE_ROWS = PAGE_SIZE * B_HEADS
GDN_SEQS_PER_STEP = 4
ROW_TILE = 512
VMEM_LIMIT = 56 * 1024 * 1024

NN = (((1,), (0,)), ((), ()))
NT = (((1,), (1,)), ((), ()))
TN = (((0,), (0,)), ((), ()))


def _dot(a, b, dims=NN):
    return lax.dot_general(a, b, dims, preferred_element_type=F32)


def _silu(x):
    return x * (1.0 / (1.0 + jnp.exp(-x)))


def _softplus(x):
    return jnp.maximum(x, 0.0) + jnp.log(1.0 + jnp.exp(-jnp.abs(x)))


LOG2E = 1.0 / math.log(2.0)


def _softplus2(x2):
    return jnp.maximum(x2, 0.0) + jnp.log(1.0 + jnp.exp2(-jnp.abs(x2))) * LOG2E


def _split_bf16(x):
    hi = x.astype(BF16)
    lo = (x - hi.astype(F32)).astype(BF16)
    return hi, lo


def _dot3(a, b, dims=NN):
    ah, al = _split_bf16(a)
    bh, bl = _split_bf16(b)
    return _dot(ah, bh, dims) + (_dot(ah, bl, dims) + _dot(al, bh, dims))


def _dot_sel(sel, x):
    s = sel.astype(BF16)
    h1 = x.astype(BF16)
    r1 = x - h1.astype(F32)
    h2 = r1.astype(BF16)
    h3 = (r1 - h2.astype(F32)).astype(BF16)
    return _dot(s, h1) + (_dot(s, h2) + _dot(s, h3))


def _params(*sem):
    return pltpu.CompilerParams(dimension_semantics=sem, vmem_limit_bytes=VMEM_LIMIT)


def _rows(tr, width):
    return pl.BlockSpec((tr, width), lambda i: (i, 0))


def _resident(shape):
    nd = len(shape)
    return pl.BlockSpec(shape, lambda *_: (0,) * nd)


_L0_COLS = (("qkv", A_QKV), ("z", A_WIDTH), ("bq", B_WIDTH), ("bk", B_WIDTH), ("bv", B_WIDTH),
            ("bz", B_WIDTH))


def _inproj0_kernel(x_ref, g_ref, w_ref, wba_ref, qkv_ref, z_ref, bq_ref, bk_ref, bv_ref, bz_ref,
                    ba_ref, bkh_ref, bvh_ref):
    x = x_ref[...]
    xn = x * lax.rsqrt(jnp.mean(x * x, axis=-1, keepdims=True) + EPS) * g_ref[...]
    xb = xn.astype(BF16)
    off = 0
    outs = {}
    for (name, width), ref in zip(_L0_COLS, (qkv_ref, z_ref, bq_ref, bk_ref, bv_ref, bz_ref)):
        u = _dot(xb, w_ref[:, off:off + width])
        outs[name] = u
        if name == "bq":
            u = u * (B_DIM ** -0.5 * LOG2E)
        ref[...] = u.astype(ref.dtype)
        off += width
    bkh_ref[...] = outs["bk"].astype(BF16)
    bvh_ref[...] = outs["bv"].astype(BF16)
    ba_ref[...] = _dot3(xn, wba_ref[...])


def _inproj0(x, g, w_main, w_ba, tr):
    n = x.shape[0]
    widths = [w for _, w in _L0_COLS]
    out_shape = [jax.ShapeDtypeStruct((n, A_QKV), F32), jax.ShapeDtypeStruct((n, A_WIDTH), F32),
                 jax.ShapeDtypeStruct((n, B_WIDTH), BF16), jax.ShapeDtypeStruct((n, B_WIDTH), F32),
                 jax.ShapeDtypeStruct((n, B_WIDTH), F32), jax.ShapeDtypeStruct((n, B_WIDTH), F32),
                 jax.ShapeDtypeStruct((n, LANES), F32),
                 jax.ShapeDtypeStruct((n, B_WIDTH), BF16), jax.ShapeDtypeStruct((n, B_WIDTH), BF16)]
    out_specs = [_rows(tr, w) for w in widths] + [_rows(tr, LANES), _rows(tr, B_WIDTH), _rows(tr, B_WIDTH)]
    return pl.pallas_call(
        _inproj0_kernel,
        grid=(n // tr,),
        in_specs=[_rows(tr, D_MODEL), _resident((1, D_MODEL)), _resident(w_main.shape), _resident(w_ba.shape)],
        out_specs=out_specs,
        out_shape=out_shape,
        compiler_params=_params("parallel"),
        name="inproj0",
    )(x, g, w_main, w_ba)


def _gdn_chunks(items, alog_row, dtb_row, onorm_row):
    C = items[0][0].shape[0]
    ri = lax.broadcasted_iota(jnp.int32, (C, C), 0)
    ci = lax.broadcasted_iota(jnp.int32, (C, C), 1)
    incl = ri >= ci
    strict = ri > ci
    eye = ri == ci
    eye_f = eye.astype(F32)
    ones = jnp.ones((C, C), F32)
    ch = []
    for c, ba, z, s_list in items:
        beta_all = 1.0 / (1.0 + jnp.exp(-ba))
        g_all = -jnp.exp(alog_row) * _softplus(ba + dtb_row)
        gc_all = _dot_sel(incl, g_all)
        gl_all = gc_all[C - 1:C, :]
        for h in range(A_HEADS):
            q = c[:, h * A_DK:(h + 1) * A_DK]
            k = c[:, (A_HEADS + h) * A_DK:(A_HEADS + h + 1) * A_DK]
            ch.append(dict(
                q=q * lax.rsqrt(jnp.sum(q * q, axis=-1, keepdims=True) + EPS) * (A_DK ** -0.5),
                k=k * lax.rsqrt(jnp.sum(k * k, axis=-1, keepdims=True) + EPS),
                v=c[:, 2 * A_HEADS * A_DK + h * A_DV:2 * A_HEADS * A_DK + (h + 1) * A_DV],
                beta=beta_all[:, h:h + 1], gc=gc_all[:, A_HEADS + h:A_HEADS + h + 1],
                gl=gl_all[:, A_HEADS + h:A_HEADS + h + 1], s=s_list[h], z=z[:, h * A_DV:(h + 1) * A_DV]))
    for x in ch:
        gcb = jnp.broadcast_to(x["gc"], (C, C))
        gcr = _dot_sel(ones, jnp.where(eye, gcb, 0.0))
        x["decay"] = jnp.exp(jnp.where(incl, gcb - gcr, NEG))
        x["kb"] = x["k"] * x["beta"]
    for x in ch:
        x["p"] = -jnp.where(strict, _dot3(x["kb"], x["k"], NT) * x["decay"], 0.0)
        x["t"] = eye_f + x["p"]
    n = 2
    while n < C:
        for x in ch:
            x["p"] = _dot3(x["p"], x["p"])
        for x in ch:
            x["t"] = x["t"] + _dot3(x["t"], x["p"])
        n *= 2
    for x in ch:
        x["eg"] = jnp.exp(x["gc"])
        x["u"] = _dot3(x["t"], x["v"] * x["beta"])
        x["w"] = _dot3(x["t"], x["kb"] * x["eg"])
        x["a_qk"] = _dot3(x["q"], x["k"], NT) * x["decay"]
    for x in ch:
        x["v_new"] = x["u"] - _dot3(x["w"], x["s"])
    for x in ch:
        kd = x["k"] * jnp.exp(x["gl"] - x["gc"])
        x["s_new"] = x["s"] * jnp.exp(x["gl"]) + _dot3(kd, x["v_new"], TN)
    for x in ch:
        o = _dot3(x["q"] * x["eg"], x["s"]) + _dot3(x["a_qk"], x["v_new"])
        o = o * lax.rsqrt(jnp.mean(o * o, axis=-1, keepdims=True) + EPS) * onorm_row
        x["o"] = o * _silu(x["z"])
    results = []
    for i in range(len(items)):
        mine = ch[i * A_HEADS:(i + 1) * A_HEADS]
        results.append(([x["o"] for x in mine], [x["s_new"] for x in mine]))
    return results


def _conv_silu(xbuf_ref, base, rows, convw_ref):
    acc = None
    for j in range(A_CONV):
        term = xbuf_ref[pl.ds(base - (A_CONV - 1) + j, rows), :] * convw_ref[j:j + 1, :]
        acc = term if acc is None else acc + term
    return _silu(acc)


_XPAD = 8


def _gdn_prompt_kernel(n_real_blocks, qkv_ref, z_ref, ba_ref, convw_ref, alog_ref, dtb_ref, onorm_ref,
                       o_ref, s_out_ref, xbuf_ref, s_ref):
    n = pl.program_id(0)
    bsz = qkv_ref.shape[0]
    C = A_CHUNK

    @pl.when(n == 0)
    def _():
        xbuf_ref[:, 0:_XPAD, :] = jnp.zeros((bsz, _XPAD, A_QKV), F32)
        s_ref[...] = jnp.zeros_like(s_ref)

    xbuf_ref[:, _XPAD:_XPAD + C, :] = qkv_ref[...]

    def run(base, rows):
        items = [(_conv_silu(xbuf_ref.at[b], _XPAD + base, rows, convw_ref), ba_ref[b, base:base + rows, :],
                  z_ref[b, base:base + rows, :], [s_ref[b, h] for h in range(A_HEADS)]) for b in range(bsz)]
        results = _gdn_chunks(items, alog_ref[...], dtb_ref[...], onorm_ref[...])
        for b, (outs, new_s) in enumerate(results):
            for h in range(A_HEADS):
                o_ref[b, base:base + rows, h * A_DV:(h + 1) * A_DV] = outs[h]
                s_ref[b, h] = new_s[h]

    @pl.when(n == 0)
    def _():
        o_ref[:, 0:SEQ_OFF, :] = jnp.zeros((bsz, SEQ_OFF, A_WIDTH), F32)
        run(SEQ_OFF, N_META)

    @pl.when(jnp.logical_and(n > 0, n <= n_real_blocks))
    def _():
        run(0, C)

    @pl.when(n > n_real_blocks)
    def _():
        o_ref[...] = jnp.zeros((bsz, C, A_WIDTH), F32)

    @pl.when(n == n_real_blocks)
    def _():
        s_out_ref[...] = s_ref[...]

    xbuf_ref[:, 0:_XPAD, :] = xbuf_ref[:, C:C + _XPAD, :]


def _gdn_prompt(qkv, z, ba, convw, alog_row, dtb_row, onorm_row, n_real):
    bsz, lp, _ = qkv.shape
    C = A_CHUNK
    nblk = lp // C
    blk = lambda w: pl.BlockSpec((bsz, C, w), lambda n: (0, n, 0))
    return pl.pallas_call(
        functools.partial(_gdn_prompt_kernel, n_real // C),
        grid=(nblk,),
        in_specs=[blk(A_QKV), blk(A_WIDTH), blk(LANES), _resident(convw.shape), _resident((1, LANES)),
                  _resident((1, LANES)), _resident((1, A_DV))],
        out_specs=[blk(A_WIDTH), _resident((bsz, A_HEADS, A_DK, A_DV))],
        out_shape=[jax.ShapeDtypeStruct((bsz, lp, A_WIDTH), F32),
                   jax.ShapeDtypeStruct((bsz, A_HEADS, A_DK, A_DV), F32)],
        scratch_shapes=[pltpu.VMEM((bsz, _XPAD + C, A_QKV), F32), pltpu.VMEM((bsz, A_HEADS, A_DK, A_DV), F32)],
        compiler_params=_params("arbitrary"),
        name="gdn_prompt",
    )(qkv, z, ba, convw, alog_row, dtb_row, onorm_row)


def _gdn_sample_kernel(qkv_ref, buf_ref, z_ref, ba_ref, s0_ref, convw_ref, alog_ref, dtb_ref, onorm_ref,
                       o_ref, s_out_ref, xbuf_ref):
    nb, S, _ = qkv_ref.shape
    xbuf_ref[:, 0:_XPAD, :] = jnp.zeros((nb, _XPAD, A_QKV), F32)
    xbuf_ref[:, _XPAD - (A_CONV - 1):_XPAD, :] = buf_ref[...]
    xbuf_ref[:, _XPAD:_XPAD + S, :] = qkv_ref[...]
    items = [(_conv_silu(xbuf_ref.at[b], _XPAD, S, convw_ref), ba_ref[b], z_ref[b],
              [s0_ref[b, h] for h in range(A_HEADS)]) for b in range(nb)]
    results = _gdn_chunks(items, alog_ref[...], dtb_ref[...], onorm_ref[...])
    for b, (outs, new_s) in enumerate(results):
        for h in range(A_HEADS):
            o_ref[b, :, h * A_DV:(h + 1) * A_DV] = outs[h]
            s_out_ref[b, h] = new_s[h]


def _gdn_sample(qkv, buf, z, ba, s0, convw, alog_row, dtb_row, onorm_row):
    db, S, _ = qkv.shape
    nb = GDN_SEQS_PER_STEP if db % GDN_SEQS_PER_STEP == 0 else 1
    blk = lambda r, w: pl.BlockSpec((nb, r, w), lambda b: (b, 0, 0))
    sblk = pl.BlockSpec((nb, A_HEADS, A_DK, A_DV), lambda b: (b, 0, 0, 0))
    return pl.pallas_call(
        _gdn_sample_kernel,
        grid=(db // nb,),
        in_specs=[blk(S, A_QKV), blk(A_CONV - 1, A_QKV), blk(S, A_WIDTH), blk(S, LANES), sblk,
                  _resident(convw.shape), _resident((1, LANES)), _resident((1, LANES)), _resident((1, A_DV))],
        out_specs=[blk(S, A_WIDTH), sblk],
        out_shape=[jax.ShapeDtypeStruct((db, S, A_WIDTH), F32),
                   jax.ShapeDtypeStruct((db, A_HEADS, A_DK, A_DV), F32)],
        scratch_shapes=[pltpu.VMEM((nb, _XPAD + S, A_QKV), F32)],
        compiler_params=_params("parallel"),
        name="gdn_sample",
    )(qkv, buf, z, ba, s0, convw, alog_row, dtb_row, onorm_row)


def _sb_tile(s, bias, mask, u_mat, v_blk, carry, acc):
    z = s + bias
    sp = _softplus2(z)
    log_beta = z - sp
    if mask is not None:
        sp = jnp.where(mask, sp, 0.0)
    n = sp.shape[0]
    hi, lo = _split_bf16(sp)
    r2 = _dot(jnp.concatenate([hi, lo], axis=0), u_mat)
    w = jnp.exp2(log_beta - (r2[:n] + r2[n:]) - carry)
    if mask is not None:
        w = jnp.where(mask, w, 0.0)
    acc = acc + _dot(w.astype(BF16), v_blk)
    carry = carry + jnp.sum(sp, axis=-1, keepdims=True)
    return carry, acc


def _sb_prompt_kernel(bias_ref, q_ref, k_ref, v_ref, bz_ref, u_ref, o_ref):
    qi = pl.program_id(1)
    T = SB_BLK
    u_mat = u_ref[...]
    rows = lax.broadcasted_iota(jnp.int32, (T, T), 0)
    cols = lax.broadcasted_iota(jnp.int32, (T, T), 1)
    causal = cols < rows

    def tiles(kbs, state, mask):
        heads = range(B_HEADS)
        sls = [pl.ds(h * B_DIM, B_DIM) for h in heads]
        starts = [pl.multiple_of(kb * T, T) for kb in kbs]
        chains = [(j, h) for j in range(len(kbs)) for h in heads]
        z = [_dot(q_ref[0, :, sls[h]], k_ref[0, pl.ds(starts[j], T), sls[h]], NT) + bias_ref[h] for j, h in chains]
        sp = [_softplus2(x) for x in z]
        log_beta = [x - y for x, y in zip(z, sp)]
        if mask is not None:
            sp = [jnp.where(mask, x, 0.0) for x in sp]
        splits = [_split_bf16(x) for x in sp]
        rs = (_dot(jnp.concatenate([hi for hi, _ in splits], axis=0), u_mat)
              + _dot(jnp.concatenate([lo for _, lo in splits], axis=0), u_mat))
        sums = [jnp.sum(x, axis=-1, keepdims=True) for x in sp]
        carries = [c for c, _ in state]
        accs = [a for _, a in state]
        for i, (j, h) in enumerate(chains):
            r = rs[T * i:T * (i + 1)]
            w = jnp.exp2(log_beta[i] - r - carries[h])
            if mask is not None:
                w = jnp.where(mask, w, 0.0)
            accs[h] = accs[h] + _dot(w.astype(BF16), v_ref[0, pl.ds(starts[j], T), sls[h]])
            carries[h] = carries[h] + sums[i]
        return tuple(zip(carries, accs))

    state = tuple((jnp.zeros((T, 1), F32), jnp.zeros((T, B_DIM), F32)) for _ in range(B_HEADS))
    state = tiles([qi], state, causal)
    odd = qi % 2
    state = lax.fori_loop(0, odd, lambda it, st: tiles([qi - 1], st, None), state)
    first = qi - 1 - odd
    state = lax.fori_loop(0, qi // 2, lambda it, st: tiles([first - 2 * it, first - 2 * it - 1], st, None), state)
    for h in range(B_HEADS):
        sl = pl.ds(h * B_DIM, B_DIM)
        o_ref[0, :, sl] = state[h][1] * _silu(bz_ref[0, :, sl])


def _sb_prompt(sb_bias, q, k, v, bz, u_mat):
    bsz, lp, _ = q.shape
    nq = lp // SB_BLK
    qblk = pl.BlockSpec((1, SB_BLK, B_WIDTH), lambda b, i: (b, i, 0))
    kvblk = pl.BlockSpec((1, lp, B_WIDTH), lambda b, i: (b, 0, 0))
    return pl.pallas_call(
        _sb_prompt_kernel,
        grid=(bsz, nq),
        in_specs=[pl.BlockSpec(memory_space=pltpu.SMEM), qblk, kvblk, kvblk, qblk, _resident(u_mat.shape)],
        out_specs=qblk,
        out_shape=jax.ShapeDtypeStruct((bsz, lp, B_WIDTH), F32),
        compiler_params=_params("parallel", "arbitrary"),
        name="sb_prompt",
    )(sb_bias, q, k, v, bz, u_mat)


def _sb_sample_kernel(pt_ref, bias_ref, q_ref, kn_ref, vn_ref, bz_ref, u_ref, ck_hbm, cv_hbm,
                      o_ref, carry_ref, acc_ref, kbuf_ref, vbuf_ref, kpg_ref, vpg_ref, sem_ref):
    npg = PAGES_PER_STEP
    c = pl.program_id(1)
    nch = pl.num_programs(1)
    g = pl.program_id(0) * nch + c
    n_steps = pl.num_programs(0) * nch
    ahead = PAGE_SLOTS - 1

    def page_copies(step):
        slot = step % PAGE_SLOTS
        sb = step // nch
        first = (nch - 1 - step % nch) * npg
        copies = []
        for p in range(npg):
            page = pt_ref[sb, first + p]
            copies.append(pltpu.make_async_copy(ck_hbm.at[page], kpg_ref.at[slot, p], sem_ref.at[slot, 0]))
            copies.append(pltpu.make_async_copy(cv_hbm.at[page], vpg_ref.at[slot, p], sem_ref.at[slot, 1]))
        return copies

    @pl.when(g == 0)
    def _():
        for j in range(ahead):
            for cp in page_copies(j):
                cp.start()

    @pl.when(g + ahead < n_steps)
    def _():
        for cp in page_copies(g + ahead):
            cp.start()

    for cp in page_copies(g):
        cp.wait()
    slot = g % PAGE_SLOTS
    T = SB_BLK
    R = q_ref.shape[1]
    S = R // B_HEADS
    u_mat = u_ref[...]
    q = q_ref[0]
    rows = lax.broadcasted_iota(jnp.int32, (R, T), 0)
    cols = lax.broadcasted_iota(jnp.int32, (R, T), 1)
    row_head = rows // S
    same_head = (cols % B_HEADS) == row_head
    rid = lax.broadcasted_iota(jnp.int32, (R, 1), 0) // S
    bias = jnp.zeros((R, 1), F32)
    for h in range(B_HEADS):
        bias = jnp.where(rid == h, bias_ref[h], bias)

    @pl.when(c == 0)
    def _():
        new_mask = jnp.logical_and(same_head, cols // B_HEADS < rows % S)
        carry, acc = _sb_tile(_dot(q, kn_ref[0], NT), bias, new_mask, u_mat, vn_ref[0],
                              jnp.zeros((R, 1), F32), jnp.zeros((R, B_DIM), F32))
        carry_ref[...] = jnp.broadcast_to(carry, carry_ref.shape)
        acc_ref[...] = acc

    for p in range(npg):
        kbuf_ref[p * PAGE_ROWS:(p + 1) * PAGE_ROWS, :] = kpg_ref[slot, p].astype(BF16)
        vbuf_ref[p * PAGE_ROWS:(p + 1) * PAGE_ROWS, :] = vpg_ref[slot, p].astype(BF16)
    ncol = npg * PAGE_ROWS
    nt = ncol // T
    carry_in = carry_ref[:, 0:1]
    z = _dot(q, kbuf_ref[...], NT) + bias
    sp = _softplus2(z)
    log_beta = z - sp
    match = (lax.broadcasted_iota(jnp.int32, (R, ncol), 1) % B_HEADS) == (
        lax.broadcasted_iota(jnp.int32, (R, ncol), 0) // S)
    sp = jnp.where(match, sp, 0.0)
    hi, lo = _split_bf16(sp)
    stacked = jnp.concatenate([part[:, t * T:(t + 1) * T] for t in range(nt) for part in (hi, lo)], axis=0)
    r2 = _dot(stacked, u_mat)
    later = carry_in
    ws = [None] * nt
    for t in reversed(range(nt)):
        cs = slice(t * T, (t + 1) * T)
        r = r2[2 * R * t:2 * R * t + R] + r2[2 * R * t + R:2 * R * (t + 1)]
        ws[t] = jnp.exp2(log_beta[:, cs] - r - later)
        later = later + jnp.sum(sp[:, cs], axis=-1, keepdims=True)
    w = jnp.where(match, jnp.concatenate(ws, axis=1), 0.0)
    acc = acc_ref[...] + _dot(w.astype(BF16), vbuf_ref[...])
    carry_ref[...] = jnp.broadcast_to(later, carry_ref.shape)
    acc_ref[...] = acc

    @pl.when(c == pl.num_programs(1) - 1)
    def _():
        for h in range(B_HEADS):
            sl = slice(h * B_DIM, (h + 1) * B_DIM)
            o_ref[0, :, sl] = acc[h * S:(h + 1) * S, :] * _silu(bz_ref[0, :, sl])


def _sb_sample(page_table, sb_bias, q_rows, kn_rows, vn_rows, bz, u_mat, cache_k, cache_v):
    db, R, _ = q_rows.shape
    S = R // B_HEADS
    n_pages = page_table.shape[1]
    npg = PAGES_PER_STEP
    assert n_pages % npg == 0 and PAGE_ROWS % SB_BLK == 0
    nch = n_pages // npg
    assert db * nch >= PAGE_SLOTS - 1
    per_seq = lambda r, w: pl.BlockSpec((1, r, w), lambda b, c, pt: (b, 0, 0))
    hbm = pl.BlockSpec(memory_space=pl.ANY)
    page_set = (PAGE_SLOTS, npg, PAGE_ROWS, B_DIM)
    grid_spec = pltpu.PrefetchScalarGridSpec(
        num_scalar_prefetch=1,
        grid=(db, nch),
        in_specs=[pl.BlockSpec(memory_space=pltpu.SMEM), per_seq(R, B_DIM), per_seq(SB_BLK, B_DIM),
                  per_seq(SB_BLK, B_DIM), per_seq(S, B_WIDTH), pl.BlockSpec(u_mat.shape, lambda b, c, pt: (0, 0)),
                  hbm, hbm],
        out_specs=per_seq(S, B_WIDTH),
        scratch_shapes=[pltpu.VMEM((R, LANES), F32), pltpu.VMEM((R, B_DIM), F32),
                        pltpu.VMEM((npg * PAGE_ROWS, B_DIM), BF16), pltpu.VMEM((npg * PAGE_ROWS, B_DIM), BF16),
                        pltpu.VMEM(page_set, F32), pltpu.VMEM(page_set, F32),
                        pltpu.SemaphoreType.DMA((PAGE_SLOTS, 2))],
    )
    return pl.pallas_call(
        _sb_sample_kernel,
        grid_spec=grid_spec,
        out_shape=jax.ShapeDtypeStruct((db, S, B_WIDTH), F32),
        compiler_params=_params("arbitrary", "arbitrary"),
        name="sb_sample",
    )(page_table, sb_bias, q_rows, kn_rows, vn_rows, bz, u_mat, cache_k, cache_v)


def _seg_mean_sq(x, seg_ones):
    hi, lo = _split_bf16(x * x)
    w = seg_ones.shape[0]
    parts = []
    for c0 in range(0, x.shape[1], w):
        parts.append(_dot(hi[:, c0:c0 + w], seg_ones) + _dot(lo[:, c0:c0 + w], seg_ones))
    ss = parts[0] if len(parts) == 1 else jnp.concatenate(parts, axis=1)
    return ss * (1.0 / C_DIM)


def _mid_kernel(h_ref, oa_ref, ob_ref, woa_ref, wob_ref, g_ref, w_ref, qn_ref, kn_ref, seg_ref,
                h1_ref, q_ref, k_ref, v_ref, gate_ref, kh_ref, vh_ref):
    h1 = h_ref[...] + _dot(oa_ref[...].astype(BF16), woa_ref[...]) + _dot(ob_ref[...].astype(BF16), wob_ref[...])
    h1_ref[...] = h1
    xn = h1 * lax.rsqrt(jnp.mean(h1 * h1, axis=-1, keepdims=True) + EPS) * g_ref[...]
    xb = xn.astype(BF16)
    seg = seg_ref[...]
    q = _dot(xb, w_ref[:, 0:C_WIDTH])
    q = q * lax.rsqrt(_seg_mean_sq(q, seg) + EPS) * qn_ref[...]
    q_ref[...] = q.astype(BF16)
    k = _dot(xb, w_ref[:, C_WIDTH:C_WIDTH + C_KV_WIDTH])
    k = k * lax.rsqrt(_seg_mean_sq(k, seg) + EPS) * kn_ref[...]
    k_ref[...] = k
    kh_ref[...] = k.astype(BF16)
    v = _dot(xb, w_ref[:, C_WIDTH + C_KV_WIDTH:C_WIDTH + 2 * C_KV_WIDTH])
    v_ref[...] = v
    vh_ref[...] = v.astype(BF16)
    gate_ref[...] = _silu(_dot(xb, w_ref[:, C_WIDTH + 2 * C_KV_WIDTH:2 * C_WIDTH + 2 * C_KV_WIDTH]))


def _mid(h, oa, ob, woa, wob, g, w1, qn_row, kn_row, seg, tr):
    n = h.shape[0]
    sds = jax.ShapeDtypeStruct
    return pl.pallas_call(
        _mid_kernel,
        grid=(n // tr,),
        in_specs=[_rows(tr, D_MODEL), _rows(tr, A_WIDTH), _rows(tr, B_WIDTH), _resident(woa.shape),
                  _resident(wob.shape), _resident((1, D_MODEL)), _resident(w1.shape), _resident((1, C_WIDTH)),
                  _resident((1, C_KV_WIDTH)), _resident(seg.shape)],
        out_specs=[_rows(tr, D_MODEL), _rows(tr, C_WIDTH), _rows(tr, C_KV_WIDTH), _rows(tr, C_KV_WIDTH),
                   _rows(tr, C_WIDTH), _rows(tr, C_KV_WIDTH), _rows(tr, C_KV_WIDTH)],
        out_shape=[sds((n, D_MODEL), F32), sds((n, C_WIDTH), BF16), sds((n, C_KV_WIDTH), F32),
                   sds((n, C_KV_WIDTH), F32), sds((n, C_WIDTH), F32), sds((n, C_KV_WIDTH), BF16),
                   sds((n, C_KV_WIDTH), BF16)],
        compiler_params=_params("parallel"),
        name="mid",
    )(h, oa, ob, woa, wob, g, w1, qn_row, kn_row, seg)


def _swa_heads(qs, pieces_per_head, sinks):
    heads = range(len(qs))
    scores = [[_dot(qs[h], k, NT) * (C_DIM ** -0.5) + b for k, _, b in pieces_per_head[h]] for h in heads]
    mx = []
    for h in heads:
        m = sinks[h]
        for s in scores[h]:
            m = jnp.maximum(m, jnp.max(s, axis=-1, keepdims=True))
        mx.append(m)
    probs = [[jnp.exp(s - mx[h]) for s in scores[h]] for h in heads]
    outs = []
    for h in heads:
        denom = jnp.exp(sinks[h] - mx[h])
        o = None
        for p, (_, v, _) in zip(probs[h], pieces_per_head[h]):
            denom = denom + jnp.sum(p, axis=-1, keepdims=True)
            t = _dot(p.astype(BF16), v)
            o = t if o is None else o + t
        outs.append(o / denom)
    return outs


def _swa_prompt_kernel(sink_ref, mbias_ref, q_ref, kp_ref, ko_ref, k0_ref, vp_ref, vo_ref, v0_ref, gate_ref,
                       tbl_ref, o_ref):
    qi = pl.program_id(1)
    col = lax.broadcasted_iota(jnp.int32, (1, BLK), 1)
    own_ok = col >= jnp.where(qi == 0, SEQ_OFF, 0)
    prev_ok = col >= jnp.where(qi == 0, BLK, jnp.where(qi == 1, SEQ_OFF, 0))
    row = lax.broadcasted_iota(jnp.int32, (BLK, N_META), 0)
    mcol = lax.broadcasted_iota(jnp.int32, (BLK, N_META), 1)
    meta_ok = qi * BLK + row - (SEQ_OFF + mcol) >= WINDOW
    sls = [slice(h * C_DIM, (h + 1) * C_DIM) for h in range(C_HEADS)]
    pieces = []
    for h in range(C_HEADS):
        g = h // (C_HEADS // C_KV_HEADS)
        kv = slice(g * C_DIM, (g + 1) * C_DIM)
        pieces.append([
            (kp_ref[0, :, kv], vp_ref[0, :, kv], jnp.where(prev_ok, tbl_ref[h, :, 0:BLK], NEG)),
            (ko_ref[0, :, kv], vo_ref[0, :, kv], jnp.where(own_ok, tbl_ref[h, :, BLK:2 * BLK], NEG)),
            (k0_ref[0, SEQ_OFF:SEQ_OFF + N_META, kv], v0_ref[0, SEQ_OFF:SEQ_OFF + N_META, kv],
             jnp.where(meta_ok, mbias_ref[h], NEG)),
        ])
    outs = _swa_heads([q_ref[0, :, sl] for sl in sls], pieces, [sink_ref[h] for h in range(C_HEADS)])
    for h, sl in enumerate(sls):
        o_ref[0, :, sl] = (outs[h] * gate_ref[0, :, sl]).astype(o_ref.dtype)


def _swa_prompt(sinks, meta_bias, q, k, v, gate, tbl):
    bsz, lp, _ = q.shape
    nq = lp // BLK
    qblk = pl.BlockSpec((1, BLK, C_WIDTH), lambda b, i: (b, i, 0))
    own = pl.BlockSpec((1, BLK, C_KV_WIDTH), lambda b, i: (b, i, 0))
    prev = pl.BlockSpec((1, BLK, C_KV_WIDTH), lambda b, i: (b, jnp.maximum(i - 1, 0), 0))
    first = pl.BlockSpec((1, BLK, C_KV_WIDTH), lambda b, i: (b, 0, 0))
    smem = pl.BlockSpec(memory_space=pltpu.SMEM)
    return pl.pallas_call(
        _swa_prompt_kernel,
        grid=(bsz, nq),
        in_specs=[smem, smem, qblk, prev, own, first, prev, own, first, qblk, _resident(tbl.shape)],
        out_specs=qblk,
        out_shape=jax.ShapeDtypeStruct((bsz, lp, C_WIDTH), BF16),
        compiler_params=_params("parallel", "parallel"),
        name="swa_prompt",
    )(sinks, meta_bias, q, k, k, k, v, v, v, gate, tbl)


def _swa_sample_kernel(sink_ref, q_ref, km_ref, kw_ref, kn_ref, vm_ref, vw_ref, vn_ref, gate_ref,
                       tm_ref, tw_ref, tn_ref, o_ref):
    sls = [slice(h * C_DIM, (h + 1) * C_DIM) for h in range(C_HEADS)]
    kvs = [slice(g * C_DIM, (g + 1) * C_DIM) for g in range(C_KV_HEADS)]
    cached = [[(km_ref[0, :, kv].astype(BF16), vm_ref[0, :, kv].astype(BF16)),
               (kw_ref[0, :, kv].astype(BF16), vw_ref[0, :, kv].astype(BF16)),
               (kn_ref[0, :, kv], vn_ref[0, :, kv])] for kv in kvs]
    pieces = []
    for h in range(C_HEADS):
        g = h // (C_HEADS // C_KV_HEADS)
        pieces.append([(k, v, t[h]) for (k, v), t in zip(cached[g], (tm_ref, tw_ref, tn_ref))])
    outs = _swa_heads([q_ref[0, :, sl] for sl in sls], pieces, [sink_ref[h] for h in range(C_HEADS)])
    for h, sl in enumerate(sls):
        o_ref[0, :, sl] = (outs[h] * gate_ref[0, :, sl]).astype(o_ref.dtype)


def _swa_sample(sinks, q, km, kw, kn, vm, vw, vn, gate, tm, tw, tn):
    db, S, _ = q.shape
    blk = lambda r, w: pl.BlockSpec((1, r, w), lambda b: (b, 0, 0))
    return pl.pallas_call(
        _swa_sample_kernel,
        grid=(db,),
        in_specs=[pl.BlockSpec(memory_space=pltpu.SMEM), blk(S, C_WIDTH),
                  blk(N_META, C_KV_WIDTH), blk(WINDOW, C_KV_WIDTH), blk(S, C_KV_WIDTH),
                  blk(N_META, C_KV_WIDTH), blk(WINDOW, C_KV_WIDTH), blk(S, C_KV_WIDTH),
                  blk(S, C_WIDTH), _resident(tm.shape), _resident(tw.shape), _resident(tn.shape)],
        out_specs=blk(S, C_WIDTH),
        out_shape=jax.ShapeDtypeStruct((db, S, C_WIDTH), BF16),
        compiler_params=_params("parallel"),
        name="swa_sample",
    )(sinks, q, km, kw, kn, vm, vw, vn, gate, tm, tw, tn)


def _outproj_kernel(h_ref, o_ref, w_ref, y_ref):
    y_ref[...] = h_ref[...] + _dot(o_ref[...], w_ref[...])


def _outproj(h, o, w, tr):
    n = h.shape[0]
    return pl.pallas_call(
        _outproj_kernel,
        grid=(n // tr,),
        in_specs=[_rows(tr, D_MODEL), _rows(tr, C_WIDTH), _resident(w.shape)],
        out_specs=_rows(tr, D_MODEL),
        out_shape=jax.ShapeDtypeStruct((n, D_MODEL), F32),
        compiler_params=_params("parallel"),
        name="outproj1",
    )(h, o, w)


def _t5_bucket(d):
    d = np.maximum(np.asarray(d, np.int64), 0)
    exact = N_BUCKETS // 2
    large = exact + (np.log(np.maximum(d, 1) / exact) / math.log(MAX_DISTANCE / exact)
                     * (N_BUCKETS - exact)).astype(np.int64)
    return np.where(d < exact, d, np.minimum(large, N_BUCKETS - 1))


def _bias_table(rel_bias, d, allow):
    onehot = (_t5_bucket(d)[None] == np.arange(N_BUCKETS)[:, None, None]) & np.asarray(allow)[None]
    b = jnp.einsum("nh,nqk->hqk", rel_bias.astype(F32), jnp.asarray(onehot, F32), precision=HIGHEST)
    return jnp.where(jnp.asarray(np.asarray(allow))[None], b, NEG)


def _row_tile(n, cap):
    t = cap
    while n % t:
        t //= 2
    return t


def kernel(x_prompt, x_sample, cache_sb_k, cache_sb_v, page_table, state_gdn, state_gdn_conv, cache_swa_k,
           cache_swa_v, cache_swa_meta_k, cache_swa_meta_v, meta_tokens, rel_bias, norm0, w_in0, conv0, a_log0,
           dt_bias0, gdn_norm0, sb_bias0, w_out0, norm1, w_in1, q_norm1, k_norm1, sinks1, w_out1):
    bsz, n_real, _ = x_prompt.shape
    db, dec_seq, _ = x_sample.shape
    past_len = page_table.shape[1] * PAGE_SIZE
    assert n_real % SB_BLK == 0 and SB_BLK % BLK == 0 and BLK % A_CHUNK == 0
    seq_end = SEQ_OFF + N_META + n_real
    lp = n_real + SB_BLK

    a_end = A_QKV + A_WIDTH
    w0_main = jnp.concatenate([w_in0[:, :a_end], w_in0[:, a_end + 2 * A_HEADS:]], axis=1).astype(BF16)
    w0_ba = jnp.pad(w_in0[:, a_end:a_end + 2 * A_HEADS], ((0, 0), (0, LANES - 2 * A_HEADS)))
    lane_row = lambda vec, off: jnp.zeros((1, LANES), F32).at[0, off:off + vec.shape[0]].set(vec.astype(F32))
    alog_row = lane_row(a_log0, A_HEADS)
    dtb_row = lane_row(dt_bias0, A_HEADS)
    onorm_row = gdn_norm0.astype(F32)[None]
    g0 = norm0.astype(F32)[None]
    g1 = norm1.astype(F32)[None]
    woa = w_out0[:A_WIDTH].astype(BF16)
    wob = w_out0[A_WIDTH:].astype(BF16)
    w1 = w_in1.astype(BF16)
    wo1 = w_out1.astype(BF16)
    qn_row = jnp.tile(q_norm1.astype(F32), C_HEADS)[None]
    kn_row = jnp.tile(k_norm1.astype(F32), C_KV_HEADS)[None]
    seg_w = 2 * LANES
    seg = (jnp.arange(seg_w)[:, None] // C_DIM == jnp.arange(seg_w)[None, :] // C_DIM).astype(BF16)
    tj = jnp.arange(SB_BLK)[:, None]
    tc = jnp.arange(SB_BLK)[None, :]
    u_prompt = (tj > tc).astype(BF16)
    u_paged = ((tj % B_HEADS == tc % B_HEADS) & (tj // B_HEADS > tc // B_HEADS)).astype(BF16)
    sb_bias = sb_bias0.astype(F32) * LOG2E
    sinks = sinks1.astype(F32)
    convw = conv0.astype(F32)

    i = np.arange(BLK)[:, None]
    c = np.arange(2 * BLK)[None, :]
    d_win = i + BLK - c
    tbl_p = _bias_table(rel_bias, d_win, (d_win >= 0) & (d_win < WINDOW))
    meta_bias = rel_bias[int(_t5_bucket(WINDOW))].astype(F32)
    t_pos = past_len + np.arange(dec_seq)[:, None]
    s_meta = np.arange(N_META)[None, :]
    s_win = past_len - WINDOW + np.arange(WINDOW)[None, :]
    s_new = past_len + np.arange(dec_seq)[None, :]
    tbl_sm = _bias_table(rel_bias, t_pos - s_meta, (t_pos - s_meta) >= WINDOW)
    in_win = lambda d: (d >= 0) & (d < WINDOW)
    tbl_sw = _bias_table(rel_bias, t_pos - s_win, in_win(t_pos - s_win))
    tbl_sn = _bias_table(rel_bias, t_pos - s_new, in_win(t_pos - s_new))

    hp = jnp.concatenate([jnp.zeros((bsz, SEQ_OFF, D_MODEL), F32),
                          jnp.broadcast_to(meta_tokens.astype(F32)[None], (bsz, N_META, D_MODEL)),
                          x_prompt, jnp.zeros((bsz, lp - seq_end, D_MODEL), F32)], axis=1)
    hp = hp.reshape(bsz * lp, D_MODEL)
    hs = x_sample.reshape(db * dec_seq, D_MODEL)
    trp = _row_tile(bsz * lp, ROW_TILE)
    trs = _row_tile(db * dec_seq, ROW_TILE)

    qkv_p, z_p, bq_p, bk_p, bv_p, bz_p, ba_p, bkh_p, bvh_p = _inproj0(hp, g0, w0_main, w0_ba, trp)
    qkv_s, z_s, bq_s, bk_s, bv_s, bz_s, ba_s, bkh_s, bvh_s = _inproj0(hs, g0, w0_main, w0_ba, trs)
    seq = lambda a: a.reshape(bsz, lp, a.shape[-1])
    dec = lambda a: a.reshape(db, dec_seq, a.shape[-1])

    oa_p, gdn_p = _gdn_prompt(seq(qkv_p), seq(z_p), seq(ba_p), convw, alog_row, dtb_row, onorm_row, n_real)
    oa_s, gdn_s = _gdn_sample(dec(qkv_s), state_gdn_conv.astype(F32), dec(z_s), dec(ba_s), state_gdn.astype(F32),
                              convw, alog_row, dtb_row, onorm_row)
    ob_p = _sb_prompt(sb_bias, seq(bq_p), seq(bkh_p), seq(bvh_p), seq(bz_p), u_prompt)
    q_rows = bq_s.reshape(db, dec_seq, B_HEADS, B_DIM).transpose(0, 2, 1, 3).reshape(db, B_HEADS * dec_seq, B_DIM)
    pad_rows = lambda a: jnp.pad(a.reshape(db, dec_seq * B_HEADS, B_DIM),
                                 ((0, 0), (0, SB_BLK - dec_seq * B_HEADS), (0, 0)))
    pages = lambda a: a.reshape(a.shape[0], PAGE_ROWS, B_DIM)
    ob_s = _sb_sample(page_table, sb_bias, q_rows, pad_rows(bkh_s), pad_rows(bvh_s), dec(bz_s), u_paged,
                      pages(cache_sb_k), pages(cache_sb_v))

    h1_p, q1_p, k1_p, v1_p, gate_p, k1h_p, v1h_p = _mid(hp, oa_p.reshape(bsz * lp, A_WIDTH),
                                                         ob_p.reshape(bsz * lp, B_WIDTH), woa, wob, g1, w1,
                                                         qn_row, kn_row, seg, trp)
    h1_s, q1_s, k1_s, v1_s, gate_s, k1h_s, v1h_s = _mid(hs, oa_s.reshape(db * dec_seq, A_WIDTH),
                                                         ob_s.reshape(db * dec_seq, B_WIDTH), woa, wob, g1, w1,
                                                         qn_row, kn_row, seg, trs)

    oc_p = _swa_prompt(sinks, meta_bias, seq(q1_p), seq(k1h_p), seq(v1h_p), seq(gate_p), tbl_p)
    flat_kv = lambda a: a.reshape(a.shape[0], a.shape[1], C_KV_WIDTH)
    oc_s = _swa_sample(sinks, dec(q1_s), flat_kv(cache_swa_meta_k), flat_kv(cache_swa_k), dec(k1h_s),
                       flat_kv(cache_swa_meta_v), flat_kv(cache_swa_v), dec(v1h_s), dec(gate_s),
                       tbl_sm, tbl_sw, tbl_sn)
    y_p = _outproj(h1_p, oc_p.reshape(bsz * lp, C_WIDTH), wo1, trp)
    y_s = _outproj(h1_s, oc_s.reshape(db * dec_seq, C_WIDTH), wo1, trs)

    first = SEQ_OFF
    real0 = SEQ_OFF + N_META
    y_prompt = seq(y_p)[:, real0:seq_end]
    y_sample = dec(y_s)
    heads_b = lambda a: a.reshape(a.shape[0], a.shape[1], B_HEADS, B_DIM)
    heads_c = lambda a: a.reshape(a.shape[0], a.shape[1], C_KV_HEADS, C_DIM)
    sbk_p = heads_b(seq(bk_p)[:, first:seq_end])
    sbv_p = heads_b(seq(bv_p)[:, first:seq_end])
    sbk_s = heads_b(dec(bk_s))
    sbv_s = heads_b(dec(bv_s))
    conv_p = seq(qkv_p)[:, seq_end - (A_CONV - 1):seq_end]
    conv_s = jnp.concatenate([state_gdn_conv.astype(F32), dec(qkv_s)], axis=1)[:, -(A_CONV - 1):]
    k1_seq, v1_seq = seq(k1_p), seq(v1_p)
    swk_p = heads_c(k1_seq[:, seq_end - WINDOW:seq_end])
    swv_p = heads_c(v1_seq[:, seq_end - WINDOW:seq_end])
    swmk_p = heads_c(k1_seq[:, first:first + N_META])
    swmv_p = heads_c(v1_seq[:, first:first + N_META])
    swk_s = jnp.concatenate([cache_swa_k.astype(F32), heads_c(dec(k1_s))], axis=1)[:, -WINDOW:]
    swv_s = jnp.concatenate([cache_swa_v.astype(F32), heads_c(dec(v1_s))], axis=1)[:, -WINDOW:]
    return (y_prompt, y_sample, sbk_p, sbv_p, sbk_s, sbv_s, gdn_p, gdn_s, conv_p, conv_s,
            swk_p, swv_p, swmk_p, swmv_p, swk_s, swv_s)
```
